```python
import math
import functools
import jax
import jax.numpy as jnp
from jax import lax
import numpy as np

D_MODEL = 2048
BATCH = 1
SEQ = 16384
DEPTH = 2

CTX_LEN = 256
GRID_W = 64
N_MIXERS = 4
GROUP_WIDTH = D_MODEL // N_MIXERS
D_MIX = N_MIXERS * GROUP_WIDTH
GDN_DK = 64
GDN_DV = 64
GDN_HEADS = GROUP_WIDTH // GDN_DV
SHORT_CONV = 3
SCAN_CHUNK = 64
SC_WIDTH = GROUP_WIDTH
GLA_DK = 64
GLA_DV = 128
GLA_HEADS = GROUP_WIDTH // GLA_DV
GLA_GATE_RANK = 16
GLA_TAU = 16.0
DIFF_DK = 64
DIFF_DV = 128
DIFF_HEADS = GROUP_WIDTH // DIFF_DV
Q_BLOCK = 128
ROPE_BASE = 10000.0
ROPE_PAIRS = DIFF_DK // 4
N_GROUPS = 4
EXPERTS_PER_GROUP = 8
N_EXPERTS = N_GROUPS * EXPERTS_PER_GROUP
TOP_K = 2
D_EXPERT = D_MODEL // 4
EXPERT_BLOCK = 256
NORM_EPS = 1e-6
F32 = jnp.float32

PROJ_LAYOUT = (
    ('gdn_qkv', GDN_HEADS * (2 * GDN_DK + GDN_DV)),
    ('gdn_gate', GDN_HEADS * GDN_DV),
    ('gdn_beta', 2 * GDN_HEADS),
    ('gdn_alpha', 2 * GDN_HEADS),
    ('sc_b', SC_WIDTH),
    ('sc_c', SC_WIDTH),
    ('sc_x', SC_WIDTH),
    ('gla_q', GLA_HEADS * GLA_DK),
    ('gla_k', GLA_HEADS * GLA_DK),
    ('gla_v', GLA_HEADS * GLA_DV),
    ('gla_r', GLA_HEADS * GLA_DV),
    ('gla_lowrank', 2 * GLA_GATE_RANK),
    ('diff_q', DIFF_HEADS * 2 * DIFF_DK),
    ('diff_k', DIFF_HEADS * 2 * DIFF_DK),
    ('diff_v', DIFF_HEADS * DIFF_DV),
)
PROJ_WIDTH = sum(size for _, size in PROJ_LAYOUT)

kernel_name = 'hybrid_dit_delta_conv_gla_diffattn_hmoe'


def rms_norm(x, gain):
    xf = x.astype(F32)
    y = xf * lax.rsqrt(jnp.mean(xf * xf, axis=-1, keepdims=True) + NORM_EPS)
    return (y * gain.astype(F32)).astype(x.dtype)


def l2_normalize(x):
    xf = x.astype(F32)
    return (xf * lax.rsqrt(jnp.sum(xf * xf, axis=-1, keepdims=True) + NORM_EPS)).astype(x.dtype)


def adaln(h, shift, scale):
    return h * (1.0 + scale) + shift


def dwconv_centred(x, w):
    pad = w.shape[0] // 2
    return lax.conv_general_dilated(
        x, w[:, None, :].astype(x.dtype), window_strides=(1,), padding=((pad, pad),),
        dimension_numbers=('NWC', 'WIO', 'NWC'), feature_group_count=x.shape[-1])


def split_proj(p):
    parts, off = {}, 0
    for name, size in PROJ_LAYOUT:
        parts[name] = p[..., off:off + size]
        off += size
    return parts


def to_chunks(t, chunk):
    bsz, T, H = t.shape[:3]
    t = t.astype(F32).reshape((bsz, T // chunk, chunk, H) + t.shape[3:])
    return jnp.moveaxis(t, (1, 3), (0, 2))


def from_chunks(t):
    t = jnp.moveaxis(t, (0, 2), (1, 3))
    return t.reshape((t.shape[0], t.shape[1] * t.shape[2]) + t.shape[3:])


def gated_delta_chunked(q, k, v, log_decay, beta, state0):
    C = SCAN_CHUNK
    q = to_chunks(q, C) * (q.shape[-1] ** -0.5)
    k = to_chunks(k, C)
    v = to_chunks(v, C)
    beta = to_chunks(beta, C)
    g = jnp.cumsum(to_chunks(log_decay, C), axis=-1)
    incl = jnp.tril(jnp.ones((C, C), bool))
    strict = jnp.tril(jnp.ones((C, C), bool), -1)
    decay = jnp.exp(jnp.where(incl, g[..., :, None] - g[..., None, :], -jnp.inf))
    kb = k * beta[..., None]
    lower = jnp.where(strict, jnp.einsum('nbhid,nbhjd->nbhij', kb, k) * decay, 0.0)
    tmat = lower + jnp.eye(C, dtype=F32)
    solve = functools.partial(lax.linalg.triangular_solve, left_side=True, lower=True, unit_diagonal=True)
    u = solve(tmat, v * beta[..., None])
    w = solve(tmat, kb * jnp.exp(g)[..., None])
    attn = jnp.einsum('nbhid,nbhjd->nbhij', q, k) * decay
    g_last = g[..., -1]
    q_dec = q * jnp.exp(g)[..., None]
    k_dec = k * jnp.exp(g_last[..., None] - g)[..., None]

    def step(S, xs):
        u_n, w_n, q_n, k_n, a_n, gl_n = xs
        v_new = u_n - jnp.einsum('bhcd,bhde->bhce', w_n, S)
        o_n = jnp.einsum('bhcd,bhde->bhce', q_n, S) + jnp.einsum('bhij,bhje->bhie', a_n, v_new)
        S = S * jnp.exp(gl_n)[..., None, None] + jnp.einsum('bhcd,bhce->bhde', k_n, v_new)
        return S, o_n

    S, o = lax.scan(step, state0, (u, w, q_dec, k_dec, attn, g_last))
    return from_chunks(o), S


def gla_chunked(q, k, v, log_a, state0):
    C = SCAN_CHUNK
    q = to_chunks(q, C) * (q.shape[-1] ** -0.5)
    k = to_chunks(k, C)
    v = to_chunks(v, C)
    b = jnp.cumsum(to_chunks(log_a, C), axis=-2)
    b_last = b[..., -1:, :]
    k_out = k * jnp.exp(b_last - b)
    incl = jnp.tril(jnp.ones((C, C), bool))
    attn = jnp.where(incl, jnp.einsum('nbhid,nbhjd->nbhij', q * jnp.exp(b - b_last), k_out), 0.0)
    contrib = jnp.einsum('nbhcd,nbhce->nbhde', k_out, v)

    def step(S, xs):
        a, dS = xs
        return a[..., None] * S + dS, S

    S, S_start = lax.scan(step, state0, (jnp.exp(b_last[..., 0, :]), contrib))
    o = jnp.einsum('nbhij,nbhje->nbhie', attn, v) + jnp.einsum('nbhcd,nbhde->nbhce', q * jnp.exp(b), S_start)
    return from_chunks(o), S


def bidir_prefix_scan(scan_fn, ctx_dirs, lat_dirs, state0):
    o_ctx, o_lat = 0.0, 0.0
    for d in range(2):
        rev = (lambda t: jnp.flip(t, axis=1)) if d == 1 else (lambda t: t)
        oc, s_ctx = scan_fn(*[rev(t) for t in ctx_dirs[d]], state0)
        ol, _ = scan_fn(*[rev(t) for t in lat_dirs[d]], s_ctx)
        o_ctx = o_ctx + rev(oc)
        o_lat = o_lat + rev(ol)
    return o_ctx, o_lat


def gdn_mixer(pl, pc, conv_w, a_log, dt_bias, out_norm, with_ctx_out):
    hk = GDN_HEADS * GDN_DK

    def prep(pp):
        bsz, T = pp['gdn_qkv'].shape[:2]
        qkv = jax.nn.silu(dwconv_centred(pp['gdn_qkv'], conv_w))
        q, k, v = jnp.split(qkv, [hk, 2 * hk], axis=-1)
        q = l2_normalize(q.reshape(bsz, T, GDN_HEADS, GDN_DK))
        k = l2_normalize(k.reshape(bsz, T, GDN_HEADS, GDN_DK))
        v = v.reshape(bsz, T, GDN_HEADS, GDN_DV)
        beta = jax.nn.sigmoid(pp['gdn_beta'].astype(F32)).reshape(bsz, T, 2, GDN_HEADS)
        alpha = pp['gdn_alpha'].astype(F32).reshape(bsz, T, 2, GDN_HEADS)
        log_decay = -jnp.exp(a_log.astype(F32)) * jax.nn.softplus(alpha + dt_bias.astype(F32))
        return [(q, k, v, log_decay[:, :, d], beta[:, :, d]) for d in range(2)]

    bsz = pl['gdn_qkv'].shape[0]
    s0 = jnp.zeros((bsz, GDN_HEADS, GDN_DK, GDN_DV), F32)
    o_ctx, o_lat = bidir_prefix_scan(gated_delta_chunked, prep(pc), prep(pl), s0)

    def post(o, pp):
        bsz, T = o.shape[:2]
        gate = pp['gdn_gate'].reshape(bsz, T, GDN_HEADS, GDN_DV)
        y = rms_norm(o.astype(gate.dtype), out_norm) * jax.nn.silu(gate)
        return y.reshape(bsz, T, GDN_HEADS * GDN_DV)

    return post(o_lat, pl), (post(o_ctx, pc) if with_ctx_out else None)


def short_conv_mixer(pp, conv_w):
    return pp['sc_b'] * dwconv_centred(pp['sc_c'] * pp['sc_x'], conv_w)


def gla_mixer(pl, pc, gate_up, gate_bias, out_norm, with_ctx_out):
    def prep(pp):
        bsz, T = pp['gla_q'].shape[:2]
        q = pp['gla_q'].reshape(bsz, T, GLA_HEADS, GLA_DK)
        k = pp['gla_k'].reshape(bsz, T, GLA_HEADS, GLA_DK)
        v = pp['gla_v'].reshape(bsz, T, GLA_HEADS, GLA_DV)
        lowrank = pp['gla_lowrank'].reshape(bsz, T, 2, GLA_GATE_RANK)
        z = jnp.einsum('btnr,nrk->btnk', lowrank, gate_up) + gate_bias
        log_a = (jax.nn.log_sigmoid(z.astype(F32)) / GLA_TAU).reshape(bsz, T, 2, GLA_HEADS, GLA_DK)
        return [(q, k, v, log_a[:, :, d]) for d in range(2)]

    bsz = pl['gla_q'].shape[0]
    s0 = jnp.zeros((bsz, GLA_HEADS, GLA_DK, GLA_DV), F32)
    o_ctx, o_lat = bidir_prefix_scan(gla_chunked, prep(pc), prep(pl), s0)

    def post(o, pp):
        bsz, T = o.shape[:2]
        r = pp['gla_r'].reshape(bsz, T, GLA_HEADS, GLA_DV)
        y = rms_norm(o.astype(r.dtype), out_norm) * jax.nn.silu(r)
        return y.reshape(bsz, T, GLA_HEADS * GLA_DV)

    return post(o_lat, pl), (post(o_ctx, pc) if with_ctx_out else None)


def axial_rope(x, cos_r, sin_r, cos_c, sin_c):
    r1, r2, c1, c2 = jnp.split(x, 4, axis=-1)
    ex = lambda t: t[None, :, None, None, :].astype(x.dtype)
    cr, sr, cc, sc = ex(cos_r), ex(sin_r), ex(cos_c), ex(sin_c)
    return jnp.concatenate([r1 * cr - r2 * sr, r2 * cr + r1 * sr, c1 * cc - c2 * sc, c2 * cc + c1 * sc], axis=-1)


def diff_softmax_attend(q, k, v, lam):
    s = jnp.einsum('bqhcd,bkhcd->bhcqk', q, k).astype(F32) * (q.shape[-1] ** -0.5)
    p = jax.nn.softmax(s, axis=-1)
    w = p[:, :, 0] - lam * p[:, :, 1]
    return jnp.einsum('bhqk,bkhd->bqhd', w.astype(v.dtype), v)


def diff_mixer(pl, pc, rope, q_norm, k_norm, lam_params, out_norm, lam_init, with_ctx_out):
    bsz, S = pl['diff_q'].shape[:2]
    n_ctx = pc['diff_k'].shape[1]
    qk_shape = lambda T: (bsz, T, DIFF_HEADS, 2, DIFF_DK)
    q_lat = axial_rope(rms_norm(pl['diff_q'].reshape(qk_shape(S)), q_norm), *rope)
    k_lat = axial_rope(rms_norm(pl['diff_k'].reshape(qk_shape(S)), k_norm), *rope)
    k_ctx = rms_norm(pc['diff_k'].reshape(qk_shape(n_ctx)), k_norm)
    v_lat = pl['diff_v'].reshape(bsz, S, DIFF_HEADS, DIFF_DV)
    v_ctx = pc['diff_v'].reshape(bsz, n_ctx, DIFF_HEADS, DIFF_DV)
    lq1, lk1, lq2, lk2 = lam_params.astype(F32)
    lam = jnp.exp(jnp.sum(lq1 * lk1)) - jnp.exp(jnp.sum(lq2 * lk2)) + lam_init
    k_all = jnp.concatenate([k_ctx, k_lat], axis=1)
    v_all = jnp.concatenate([v_ctx, v_lat], axis=1)
    q_blocks = jnp.moveaxis(q_lat.reshape(bsz, S // Q_BLOCK, Q_BLOCK, DIFF_HEADS, 2, DIFF_DK), 1, 0)
    o_lat = lax.map(lambda qb: diff_softmax_attend(qb, k_all, v_all, lam), q_blocks)
    o_lat = jnp.moveaxis(o_lat, 0, 1).reshape(bsz, S, DIFF_HEADS, DIFF_DV)

    def post(o):
        y = rms_norm(o, out_norm) * (1.0 - lam_init)
        return y.reshape(o.shape[0], o.shape[1], DIFF_HEADS * DIFF_DV)

    if not with_ctx_out:
        return post(o_lat), None
    q_ctx = rms_norm(pc['diff_q'].reshape(qk_shape(n_ctx)), q_norm)
    return post(o_lat), post(diff_softmax_attend(q_ctx, k_ctx, v_ctx, lam))


def grouped_expert_ffn(h, expert_idx, weights, w1, w3, w2):
    N, D = h.shape
    K = expert_idx.shape[1]
    E = w1.shape[0]
    M = EXPERT_BLOCK
    A = N * K
    flat_e = expert_idx.reshape(-1)
    flat_t = jnp.arange(A, dtype=jnp.int32) // K
    flat_w = weights.reshape(-1)
    order = jnp.argsort(flat_e)
    e_s, t_s, w_s = flat_e[order], flat_t[order], flat_w[order]
    counts = jnp.bincount(flat_e, length=E)
    padded = (counts + M - 1) // M * M
    pad_end = jnp.cumsum(padded)
    pad_start = pad_end - padded
    start = jnp.cumsum(counts) - counts
    dest = pad_start[e_s] + (jnp.arange(A, dtype=jnp.int32) - start[e_s])
    n_blocks = -(-A // M) + E
    L = n_blocks * M
    slot_tok = jnp.full((L,), N, jnp.int32).at[dest].set(t_s)
    slot_w = jnp.zeros((L,), h.dtype).at[dest].set(w_s.astype(h.dtype))
    block_expert = jnp.minimum(jnp.searchsorted(pad_end, jnp.arange(n_blocks, dtype=jnp.int32) * M, side='right'), E - 1)
    h_pad = jnp.concatenate([h, jnp.zeros((1, D), h.dtype)], axis=0)

    def run_block(args):
        tok, e = args
        xb = h_pad[tok]
        return (jax.nn.silu(xb @ w1[e]) * (xb @ w3[e])) @ w2[e]

    y = lax.map(run_block, (slot_tok.reshape(n_blocks, M), block_expert))
    y = y.reshape(L, D) * slot_w[:, None]
    return jnp.zeros((N + 1, D), h.dtype).at[slot_tok].add(y)[:N]


def hier_moe(h, wg, bg, we, be, w1, w3, w2):
    bsz, T, D = h.shape
    hf = h.reshape(-1, D)
    N = hf.shape[0]
    g_prob = jax.nn.softmax((hf @ wg + bg).astype(F32), axis=-1)
    g_top_p, g_idx = lax.top_k(g_prob, 1)
    e_logits = (hf @ we + be).astype(F32).reshape(N, N_GROUPS, EXPERTS_PER_GROUP)
    e_in_group = e_logits[jnp.arange(N), g_idx[:, 0]]
    e_top_p, e_top_i = lax.top_k(jax.nn.softmax(e_in_group, axis=-1), TOP_K)
    e_top_p = e_top_p / jnp.sum(e_top_p, axis=-1, keepdims=True)
    weights = g_top_p * e_top_p
    expert_idx = g_idx * EXPERTS_PER_GROUP + e_top_i
    return grouped_expert_ffn(hf, expert_idx, weights, w1, w3, w2).reshape(bsz, T, D)


def setup_inputs(seed: int = 0) -> dict:
    key = jax.random.key(seed)
    ks = iter(jax.random.split(key, 40))

    def nrm(shape, scale):
        return jax.random.normal(next(ks), shape, F32) * scale

    def gain(shape):
        return 1.0 + nrm(shape, 0.01)

    L = DEPTH
    gdn_qkv_w = GDN_HEADS * (2 * GDN_DK + GDN_DV)
    dt = jnp.exp(jax.random.uniform(next(ks), (L, 2, GDN_HEADS), F32, math.log(1e-3), math.log(1e-1)))
    return {
        'x': nrm((BATCH, SEQ, D_MODEL), 1.0),
        'c': nrm((BATCH, D_MODEL), 1.0),
        'ctx': nrm((BATCH, CTX_LEN, D_MODEL), 1.0),
        'c_ctx': nrm((D_MODEL,), 1.0),
        'w_mod': nrm((L, D_MODEL, 6 * D_MODEL), 0.5 * D_MODEL ** -0.5),
        'b_mod': nrm((L, 6 * D_MODEL), 0.01),
        'norm1': gain((L, D_MODEL)),
        'norm2': gain((L, D_MODEL)),
        'w_in': nrm((L, D_MODEL, PROJ_WIDTH), D_MODEL ** -0.5),
        'w_out': nrm((L, D_MIX, D_MODEL), D_MIX ** -0.5),
        'gdn_conv': nrm((L, SHORT_CONV, gdn_qkv_w), SHORT_CONV ** -0.5),
        'gdn_a_log': jnp.log(jax.random.uniform(next(ks), (L, 2, GDN_HEADS), F32, 1.0, 16.0)),
        'gdn_dt_bias': dt + jnp.log(-jnp.expm1(-dt)),
        'gdn_out_norm': gain((L, GDN_DV)),
        'sc_conv': nrm((L, SHORT_CONV, SC_WIDTH), SHORT_CONV ** -0.5),
        'gla_gate_up': nrm((L, 2, GLA_GATE_RANK, GLA_HEADS * GLA_DK), GLA_GATE_RANK ** -0.5),
        'gla_gate_bias': nrm((L, 2, GLA_HEADS * GLA_DK), 0.01),
        'gla_out_norm': gain((L, GLA_DV)),
        'diff_q_norm': gain((L, DIFF_DK)),
        'diff_k_norm': gain((L, DIFF_DK)),
        'diff_lambda': nrm((L, 4, DIFF_DK), 0.1),
        'diff_out_norm': gain((L, DIFF_DV)),
        'router_group': nrm((L, D_MODEL, N_GROUPS), D_MODEL ** -0.5),
        'router_group_bias': nrm((L, N_GROUPS), 0.01),
        'router_expert': nrm((L, D_MODEL, N_EXPERTS), D_MODEL ** -0.5),
        'router_expert_bias': nrm((L, N_EXPERTS), 0.01),
        'expert_w1': nrm((L, N_EXPERTS, D_MODEL, D_EXPERT), D_MODEL ** -0.5),
        'expert_w3': nrm((L, N_EXPERTS, D_MODEL, D_EXPERT), D_MODEL ** -0.5),
        'expert_w2': nrm((L, N_EXPERTS, D_EXPERT, D_MODEL), D_EXPERT ** -0.5),
    }


def reference(x, c, ctx, c_ctx, w_mod, b_mod, norm1, norm2, w_in, w_out, gdn_conv, gdn_a_log,
              gdn_dt_bias, gdn_out_norm, sc_conv, gla_gate_up, gla_gate_bias, gla_out_norm,
              diff_q_norm, diff_k_norm, diff_lambda, diff_out_norm, router_group, router_group_bias,
              router_expert, router_expert_bias, expert_w1, expert_w3, expert_w2):
    bsz, S, _ = x.shape
    n_ctx = ctx.shape[1]
    rows = S // GRID_W
    pos = jnp.arange(rows * GRID_W, dtype=jnp.int32)
    row = (pos // GRID_W).astype(F32)
    col = (pos % GRID_W).astype(F32)
    inv_freq = ROPE_BASE ** (-jnp.arange(ROPE_PAIRS, dtype=F32) / ROPE_PAIRS)
    ang_r, ang_c = row[:, None] * inv_freq, col[:, None] * inv_freq
    rope = (jnp.cos(ang_r), jnp.sin(ang_r), jnp.cos(ang_c), jnp.sin(ang_c))

    x_ctx = ctx
    cond, cond_ctx = jax.nn.silu(c), jax.nn.silu(c_ctx)
    for l in range(DEPTH):
        with_ctx_out = l < DEPTH - 1
        lam_init = 0.8 - 0.6 * math.exp(-0.3 * l)
        m = jnp.split((cond @ w_mod[l] + b_mod[l])[:, None, :], 6, axis=-1)
        mc = jnp.split(cond_ctx @ w_mod[l] + b_mod[l], 6, axis=-1)

        h_lat = adaln(rms_norm(x, norm1[l]), m[0], m[1])
        h_ctx = adaln(rms_norm(x_ctx, norm1[l]), mc[0], mc[1])
        p_lat = split_proj(h_lat @ w_in[l])
        p_ctx = split_proj(h_ctx @ w_in[l])
        a_lat, a_ctx = gdn_mixer(p_lat, p_ctx, gdn_conv[l], gdn_a_log[l], gdn_dt_bias[l], gdn_out_norm[l], with_ctx_out)
        b_lat = short_conv_mixer(p_lat, sc_conv[l])
        g_lat, g_ctx = gla_mixer(p_lat, p_ctx, gla_gate_up[l], gla_gate_bias[l], gla_out_norm[l], with_ctx_out)
        d_lat, d_ctx = diff_mixer(p_lat, p_ctx, rope, diff_q_norm[l], diff_k_norm[l], diff_lambda[l],
                                  diff_out_norm[l], lam_init, with_ctx_out)
        x = x + m[2] * (jnp.concatenate([a_lat, b_lat, g_lat, d_lat], axis=-1) @ w_out[l])

        moe_w = (router_group[l], router_group_bias[l], router_expert[l], router_expert_bias[l],
                 expert_w1[l], expert_w3[l], expert_w2[l])
        h2_lat = adaln(rms_norm(x, norm2[l]), m[3], m[4])
        if with_ctx_out:
            b_ctx = short_conv_mixer(p_ctx, sc_conv[l])
            x_ctx = x_ctx + mc[2] * (jnp.concatenate([a_ctx, b_ctx, g_ctx, d_ctx], axis=-1) @ w_out[l])
            h2_ctx = adaln(rms_norm(x_ctx, norm2[l]), mc[3], mc[4])
            y = hier_moe(jnp.concatenate([h2_ctx, h2_lat], axis=1), *moe_w)
            x_ctx = x_ctx + mc[5] * y[:, :n_ctx]
            x = x + m[5] * y[:, n_ctx:]
        else:
            x = x + m[5] * hier_moe(h2_lat, *moe_w)
    return x
```

```python
import functools
import math

import jax
import jax.numpy as jnp
from jax import lax
from jax.experimental import pallas as pl
from jax.experimental.pallas import tpu as pltpu

D_MODEL = 2048
SEQ = 16384
DEPTH = 2
CTX_LEN = 256
GRID_W = 64
GROUP_WIDTH = 512
GDN_DK = 64
GDN_DV = 64
GDN_HEADS = GROUP_WIDTH // GDN_DV
SCAN_CHUNK = 64
SC_WIDTH = GROUP_WIDTH
GLA_DK = 64
GLA_DV = 128
GLA_HEADS = GROUP_WIDTH // GLA_DV
GLA_GATE_RANK = 16
GLA_TAU = 16.0
DIFF_DK = 64
DIFF_DV = 128
DIFF_HEADS = GROUP_WIDTH // DIFF_DV
Q_BLOCK = 128
ROPE_BASE = 10000.0
ROPE_PAIRS = DIFF_DK // 4
N_GROUPS = 4
EXPERTS_PER_GROUP = 8
N_EXPERTS = N_GROUPS * EXPERTS_PER_GROUP
TOP_K = 2
D_EXPERT = D_MODEL // 4
EXPERT_BLOCK = 256
NORM_EPS = 1e-6
F32 = jnp.float32
BF16 = jnp.bfloat16

PROJ_LAYOUT = (
    ('gdn_qkv', GDN_HEADS * (2 * GDN_DK + GDN_DV)),
    ('gdn_gate', GDN_HEADS * GDN_DV),
    ('gdn_beta', 2 * GDN_HEADS),
    ('gdn_alpha', 2 * GDN_HEADS),
    ('sc_b', SC_WIDTH),
    ('sc_c', SC_WIDTH),
    ('sc_x', SC_WIDTH),
    ('gla_q', GLA_HEADS * GLA_DK),
    ('gla_k', GLA_HEADS * GLA_DK),
    ('gla_v', GLA_HEADS * GLA_DV),
    ('gla_r', GLA_HEADS * GLA_DV),
    ('gla_lowrank', 2 * GLA_GATE_RANK),
    ('diff_q', DIFF_HEADS * 2 * DIFF_DK),
    ('diff_k', DIFF_HEADS * 2 * DIFF_DK),
    ('diff_v', DIFF_HEADS * DIFF_DV),
)
PROJ_OFFSETS = {}
_off = 0
for _name, _size in PROJ_LAYOUT:
    PROJ_OFFSETS[_name] = (_off, _size)
    _off += _size
PROJ_WIDTH = _off

MAIN_GROUPS = ('gdn_qkv', 'gdn_gate', 'sc_b', 'sc_c', 'sc_x', 'gla_q', 'gla_k', 'gla_v', 'gla_r',
               'diff_q', 'diff_k', 'diff_v')
SMALL_GROUPS = ('gdn_beta', 'gdn_alpha', 'gla_lowrank')
MAIN_OFFSETS = {}
_off = 0
for _name in MAIN_GROUPS:
    MAIN_OFFSETS[_name] = (_off, PROJ_OFFSETS[_name][1])
    _off += PROJ_OFFSETS[_name][1]
MAIN_WIDTH = _off
SMALL_OFFSETS = {}
_off = 0
for _name in SMALL_GROUPS:
    SMALL_OFFSETS[_name] = (_off, PROJ_OFFSETS[_name][1])
    _off += PROJ_OFFSETS[_name][1]
LANE = 128
SMALL_WIDTH = LANE

VMEM_LIMIT = 48 * 1024 * 1024


INPROJ_TM = 1024
INPROJ_TN = 512


def _inproj_body(x_ref, gain_ref, shift_ref, scale_ref, w_ref, ws_ref, o_ref, os_ref, h_ref):
    @pl.when(pl.program_id(1) == 0)
    def _():
        x = x_ref[...]
        y = x * lax.rsqrt(jnp.mean(x * x, axis=-1, keepdims=True) + NORM_EPS) * gain_ref[...]
        hb = (y * (1.0 + scale_ref[...]) + shift_ref[...]).astype(BF16)
        h_ref[...] = hb
        os_ref[...] = jnp.dot(hb, ws_ref[...], preferred_element_type=F32)

    o_ref[...] = jnp.dot(h_ref[...], w_ref[...], preferred_element_type=F32)


def inproj(x, gain, shift, scale, w_main, w_small):
    rows = x.shape[0]
    tm = min(INPROJ_TM, rows)
    grid = (pl.cdiv(rows, tm), MAIN_WIDTH // INPROJ_TN)
    vec = pl.BlockSpec((1, D_MODEL), lambda i, j: (0, 0))
    return pl.pallas_call(
        _inproj_body,
        grid=grid,
        in_specs=[
            pl.BlockSpec((tm, D_MODEL), lambda i, j: (i, 0)),
            vec, vec, vec,
            pl.BlockSpec((D_MODEL, INPROJ_TN), lambda i, j: (0, j)),
            pl.BlockSpec((D_MODEL, SMALL_WIDTH), lambda i, j: (0, 0)),
        ],
        out_specs=[
            pl.BlockSpec((tm, INPROJ_TN), lambda i, j: (i, j)),
            pl.BlockSpec((tm, SMALL_WIDTH), lambda i, j: (i, 0)),
        ],
        out_shape=[
            jax.ShapeDtypeStruct((rows, MAIN_WIDTH), F32),
            jax.ShapeDtypeStruct((rows, SMALL_WIDTH), F32),
        ],
        scratch_shapes=[pltpu.VMEM((tm, D_MODEL), BF16)],
        compiler_params=pltpu.CompilerParams(
            dimension_semantics=("parallel", "arbitrary"), vmem_limit_bytes=VMEM_LIMIT),
        name="inproj",
    )(x, gain, shift, scale, w_main, w_small)


OUTPROJ_TM = 512


def _outproj_body(a_ref, x_ref, gate_ref, w_ref, o_ref):
    y = jnp.dot(a_ref[...].astype(BF16), w_ref[...], preferred_element_type=F32)
    o_ref[...] = x_ref[...] + gate_ref[...] * y


def outproj(a, x, gate, w):
    rows = x.shape[0]
    tm = min(OUTPROJ_TM, rows)
    return pl.pallas_call(
        _outproj_body,
        grid=(pl.cdiv(rows, tm),),
        in_specs=[
            pl.BlockSpec((tm, D_MODEL), lambda i: (i, 0)),
            pl.BlockSpec((tm, D_MODEL), lambda i: (i, 0)),
            pl.BlockSpec((1, D_MODEL), lambda i: (0, 0)),
            pl.BlockSpec((D_MODEL, D_MODEL), lambda i: (0, 0)),
        ],
        out_specs=pl.BlockSpec((tm, D_MODEL), lambda i: (i, 0)),
        out_shape=jax.ShapeDtypeStruct((rows, D_MODEL), F32),
        compiler_params=pltpu.CompilerParams(
            dimension_semantics=("parallel",), vmem_limit_bytes=VMEM_LIMIT),
        name="outproj",
    )(a, x, gate, w)


def rms_norm(x, gain):
    y = x * lax.rsqrt(jnp.mean(x * x, axis=-1, keepdims=True) + NORM_EPS)
    return y * gain


def l2_normalize(x):
    return x * lax.rsqrt(jnp.sum(x * x, axis=-1, keepdims=True) + NORM_EPS)


def dwconv_centred(x, w):
    pad = w.shape[0] // 2
    return lax.conv_general_dilated(
        x, w[:, None, :], window_strides=(1,), padding=((pad, pad),),
        dimension_numbers=('NWC', 'WIO', 'NWC'), feature_group_count=x.shape[-1])


def to_chunks(t, chunk):
    bsz, T, H = t.shape[:3]
    t = t.reshape((bsz, T // chunk, chunk, H) + t.shape[3:])
    return jnp.moveaxis(t, (1, 3), (0, 2))


def from_chunks(t):
    t = jnp.moveaxis(t, (0, 2), (1, 3))
    return t.reshape((t.shape[0], t.shape[1] * t.shape[2]) + t.shape[3:])


def gated_delta_chunked(q, k, v, log_decay, beta, state0):
    C = SCAN_CHUNK
    q = to_chunks(q, C) * (q.shape[-1] ** -0.5)
    k = to_chunks(k, C)
    v = to_chunks(v, C)
    beta = to_chunks(beta, C)
    g = jnp.cumsum(to_chunks(log_decay, C), axis=-1)
    incl = jnp.tril(jnp.ones((C, C), bool))
    strict = jnp.tril(jnp.ones((C, C), bool), -1)
    decay = jnp.exp(jnp.where(incl, g[..., :, None] - g[..., None, :], -jnp.inf))
    kb = k * beta[..., None]
    lower = jnp.where(strict, jnp.einsum('nbhid,nbhjd->nbhij', kb, k) * decay, 0.0)
    tmat = lower + jnp.eye(C, dtype=F32)
    solve = functools.partial(lax.linalg.triangular_solve, left_side=True, lower=True, unit_diagonal=True)
    u = solve(tmat, v * beta[..., None])
    w = solve(tmat, kb * jnp.exp(g)[..., None])
    attn = jnp.einsum('nbhid,nbhjd->nbhij', q, k) * decay
    g_last = g[..., -1]
    q_dec = q * jnp.exp(g)[..., None]
    k_dec = k * jnp.exp(g_last[..., None] - g)[..., None]

    def step(S, xs):
        u_n, w_n, q_n, k_n, a_n, gl_n = xs
        v_new = u_n - jnp.einsum('bhcd,bhde->bhce', w_n, S)
        o_n = jnp.einsum('bhcd,bhde->bhce', q_n, S) + jnp.einsum('bhij,bhje->bhie', a_n, v_new)
        S = S * jnp.exp(gl_n)[..., None, None] + jnp.einsum('bhcd,bhce->bhde', k_n, v_new)
        return S, o_n

    S, o = lax.scan(step, state0, (u, w, q_dec, k_dec, attn, g_last))
    return from_chunks(o), S


def gla_chunked(q, k, v, log_a, state0):
    C = SCAN_CHUNK
    q = to_chunks(q, C) * (q.shape[-1] ** -0.5)
    k = to_chunks(k, C)
    v = to_chunks(v, C)
    b = jnp.cumsum(to_chunks(log_a, C), axis=-2)
    b_last = b[..., -1:, :]
    k_out = k * jnp.exp(b_last - b)
    incl = jnp.tril(jnp.ones((C, C), bool))
    attn = jnp.where(incl, jnp.einsum('nbhid,nbhjd->nbhij', q * jnp.exp(b - b_last), k_out), 0.0)
    contrib = jnp.einsum('nbhcd,nbhce->nbhde', k_out, v)

    def step(S, xs):
        a, dS = xs
        return a[..., None] * S + dS, S

    S, S_start = lax.scan(step, state0, (jnp.exp(b_last[..., 0, :]), contrib))
    o = jnp.einsum('nbhij,nbhje->nbhie', attn, v) + jnp.einsum('nbhcd,nbhde->nbhce', q * jnp.exp(b), S_start)
    return from_chunks(o), S


def bidir_prefix_scan(scan_fn, ctx_dirs, lat_dirs, state0):
    o_ctx, o_lat = 0.0, 0.0
    for d in range(2):
        rev = (lambda t: jnp.flip(t, axis=1)) if d == 1 else (lambda t: t)
        oc, s_ctx = scan_fn(*[rev(t) for t in ctx_dirs[d]], state0)
        ol, _ = scan_fn(*[rev(t) for t in lat_dirs[d]], s_ctx)
        o_ctx = o_ctx + rev(oc)
        o_lat = o_lat + rev(ol)
    return o_ctx, o_lat


def gdn_mixer(pl_, pc, conv_w, a_log, dt_bias, out_norm, with_ctx_out):
    hk = GDN_HEADS * GDN_DK

    def prep(pp):
        bsz, T = pp['gdn_qkv'].shape[:2]
        qkv = jax.nn.silu(dwconv_centred(pp['gdn_qkv'], conv_w))
        q, k, v = jnp.split(qkv, [hk, 2 * hk], axis=-1)
        q = l2_normalize(q.reshape(bsz, T, GDN_HEADS, GDN_DK))
        k = l2_normalize(k.reshape(bsz, T, GDN_HEADS, GDN_DK))
        v = v.reshape(bsz, T, GDN_HEADS, GDN_DV)
        beta = jax.nn.sigmoid(pp['gdn_beta']).reshape(bsz, T, 2, GDN_HEADS)
        alpha = pp['gdn_alpha'].reshape(bsz, T, 2, GDN_HEADS)
        log_decay = -jnp.exp(a_log) * jax.nn.softplus(alpha + dt_bias)
        return [(q, k, v, log_decay[:, :, d], beta[:, :, d]) for d in range(2)]

    bsz = pl_['gdn_qkv'].shape[0]
    s0 = jnp.zeros((bsz, GDN_HEADS, GDN_DK, GDN_DV), F32)
    o_ctx, o_lat = bidir_prefix_scan(gated_delta_chunked, prep(pc), prep(pl_), s0)

    def post(o, pp):
        bsz, T = o.shape[:2]
        gate = pp['gdn_gate'].reshape(bsz, T, GDN_HEADS, GDN_DV)
        y = rms_norm(o, out_norm) * jax.nn.silu(gate)
        return y.reshape(bsz, T, GDN_HEADS * GDN_DV)

    return post(o_lat, pl_), (post(o_ctx, pc) if with_ctx_out else None)


def short_conv_mixer(pp, conv_w):
    return pp['sc_b'] * dwconv_centred(pp['sc_c'] * pp['sc_x'], conv_w)


def gla_mixer(pl_, pc, gate_up, gate_bias, out_norm, with_ctx_out):
    def prep(pp):
        bsz, T = pp['gla_q'].shape[:2]
        q = pp['gla_q'].reshape(bsz, T, GLA_HEADS, GLA_DK)
        k = pp['gla_k'].reshape(bsz, T, GLA_HEADS, GLA_DK)
        v = pp['gla_v'].reshape(bsz, T, GLA_HEADS, GLA_DV)
        lowrank = pp['gla_lowrank'].reshape(bsz, T, 2, GLA_GATE_RANK)
        z = jnp.einsum('btnr,nrk->btnk', lowrank, gate_up) + gate_bias
        log_a = (jax.nn.log_sigmoid(z) / GLA_TAU).reshape(bsz, T, 2, GLA_HEADS, GLA_DK)
        return [(q, k, v, log_a[:, :, d]) for d in range(2)]

    bsz = pl_['gla_q'].shape[0]
    s0 = jnp.zeros((bsz, GLA_HEADS, GLA_DK, GLA_DV), F32)
    o_ctx, o_lat = bidir_prefix_scan(gla_chunked, prep(pc), prep(pl_), s0)

    def post(o, pp):
        bsz, T = o.shape[:2]
        r = pp['gla_r'].reshape(bsz, T, GLA_HEADS, GLA_DV)
        y = rms_norm(o, out_norm) * jax.nn.silu(r)
        return y.reshape(bsz, T, GLA_HEADS * GLA_DV)

    return post(o_lat, pl_), (post(o_ctx, pc) if with_ctx_out else None)


def axial_rope(x, cos_r, sin_r, cos_c, sin_c):
    r1, r2, c1, c2 = jnp.split(x, 4, axis=-1)
    ex = lambda t: t[None, :, None, None, :]
    cr, sr, cc, sc = ex(cos_r), ex(sin_r), ex(cos_c), ex(sin_c)
    return jnp.concatenate([r1 * cr - r2 * sr, r2 * cr + r1 * sr, c1 * cc - c2 * sc, c2 * cc + c1 * sc], axis=-1)


def diff_softmax_attend(q, k, v, lam):
    s = jnp.einsum('bqhcd,bkhcd->bhcqk', q, k) * (q.shape[-1] ** -0.5)
    p = jax.nn.softmax(s, axis=-1)
    w = p[:, :, 0] - lam * p[:, :, 1]
    return jnp.einsum('bhqk,bkhd->bqhd', w, v)


def diff_mixer(pl_, pc, rope, q_norm, k_norm, lam_params, out_norm, lam_init, with_ctx_out):
    bsz, S = pl_['diff_q'].shape[:2]
    n_ctx = pc['diff_k'].shape[1]
    qk_shape = lambda T: (bsz, T, DIFF_HEADS, 2, DIFF_DK)
    q_lat = axial_rope(rms_norm(pl_['diff_q'].reshape(qk_shape(S)), q_norm), *rope)
    k_lat = axial_rope(rms_norm(pl_['diff_k'].reshape(qk_shape(S)), k_norm), *rope)
    k_ctx = rms_norm(pc['diff_k'].reshape(qk_shape(n_ctx)), k_norm)
    v_lat = pl_['diff_v'].reshape(bsz, S, DIFF_HEADS, DIFF_DV)
    v_ctx = pc['diff_v'].reshape(bsz, n_ctx, DIFF_HEADS, DIFF_DV)
    lq1, lk1, lq2, lk2 = lam_params
    lam = jnp.exp(jnp.sum(lq1 * lk1)) - jnp.exp(jnp.sum(lq2 * lk2)) + lam_init
    k_all = jnp.concatenate([k_ctx, k_lat], axis=1)
    v_all = jnp.concatenate([v_ctx, v_lat], axis=1)
    q_blocks = jnp.moveaxis(q_lat.reshape(bsz, S // Q_BLOCK, Q_BLOCK, DIFF_HEADS, 2, DIFF_DK), 1, 0)
    o_lat = lax.map(lambda qb: diff_softmax_attend(qb, k_all, v_all, lam), q_blocks)
    o_lat = jnp.moveaxis(o_lat, 0, 1).reshape(bsz, S, DIFF_HEADS, DIFF_DV)

    def post(o):
        y = rms_norm(o, out_norm) * (1.0 - lam_init)
        return y.reshape(o.shape[0], o.shape[1], DIFF_HEADS * DIFF_DV)

    if not with_ctx_out:
        return post(o_lat), None
    q_ctx = rms_norm(pc['diff_q'].reshape(qk_shape(n_ctx)), q_norm)
    return post(o_lat), post(diff_softmax_attend(q_ctx, k_ctx, v_ctx, lam))


def grouped_expert_ffn(h, expert_idx, weights, w1, w3, w2):
    N, D = h.shape
    K = expert_idx.shape[1]
    E = w1.shape[0]
    M = EXPERT_BLOCK
    A = N * K
    flat_e = expert_idx.reshape(-1)
    flat_t = jnp.arange(A, dtype=jnp.int32) // K
    flat_w = weights.reshape(-1)
    order = jnp.argsort(flat_e)
    e_s, t_s, w_s = flat_e[order], flat_t[order], flat_w[order]
    counts = jnp.bincount(flat_e, length=E)
    padded = (counts + M - 1) // M * M
    pad_end = jnp.cumsum(padded)
    pad_start = pad_end - padded
    start = jnp.cumsum(counts) - counts
    dest = pad_start[e_s] + (jnp.arange(A, dtype=jnp.int32) - start[e_s])
    n_blocks = -(-A // M) + E
    L = n_blocks * M
    slot_tok = jnp.full((L,), N, jnp.int32).at[dest].set(t_s)
    slot_w = jnp.zeros((L,), h.dtype).at[dest].set(w_s.astype(h.dtype))
    block_expert = jnp.minimum(jnp.searchsorted(pad_end, jnp.arange(n_blocks, dtype=jnp.int32) * M, side='right'), E - 1)
    h_pad = jnp.concatenate([h, jnp.zeros((1, D), h.dtype)], axis=0)

    def run_block(args):
        tok, e = args
        xb = h_pad[tok]
        return (jax.nn.silu(xb @ w1[e]) * (xb @ w3[e])) @ w2[e]

    y = lax.map(run_block, (slot_tok.reshape(n_blocks, M), block_expert))
    y = y.reshape(L, D) * slot_w[:, None]
    return jnp.zeros((N + 1, D), h.dtype).at[slot_tok].add(y)[:N]


def hier_moe(h, wg, bg, we, be, w1, w3, w2):
    bsz, T, D = h.shape
    hf = h.reshape(-1, D)
    N = hf.shape[0]
    g_prob = jax.nn.softmax(hf @ wg + bg, axis=-1)
    g_top_p, g_idx = lax.top_k(g_prob, 1)
    e_logits = (hf @ we + be).reshape(N, N_GROUPS, EXPERTS_PER_GROUP)
    e_in_group = e_logits[jnp.arange(N), g_idx[:, 0]]
    e_top_p, e_top_i = lax.top_k(jax.nn.softmax(e_in_group, axis=-1), TOP_K)
    e_top_p = e_top_p / jnp.sum(e_top_p, axis=-1, keepdims=True)
    weights = g_top_p * e_top_p
    expert_idx = g_idx * EXPERTS_PER_GROUP + e_top_i
    return grouped_expert_ffn(hf, expert_idx, weights, w1, w3, w2).reshape(bsz, T, D)


def _split_main(p_main, p_small):
    parts = {}
    for name, (off, size) in MAIN_OFFSETS.items():
        parts[name] = p_main[None, :, off:off + size]
    for name, (off, size) in SMALL_OFFSETS.items():
        parts[name] = p_small[None, :, off:off + size]
    return parts


def _prep_w_in(w):
    main = jnp.concatenate([w[:, PROJ_OFFSETS[n][0]:PROJ_OFFSETS[n][0] + PROJ_OFFSETS[n][1]] for n in MAIN_GROUPS], axis=1)
    small = jnp.concatenate([w[:, PROJ_OFFSETS[n][0]:PROJ_OFFSETS[n][0] + PROJ_OFFSETS[n][1]] for n in SMALL_GROUPS], axis=1)
    small = jnp.pad(small, ((0, 0), (0, SMALL_WIDTH - small.shape[1])))
    return main.astype(BF16), small.astype(BF16)


def kernel(x, c, ctx, c_ctx, w_mod, b_mod, norm1, norm2, w_in, w_out, gdn_conv, gdn_a_log, gdn_dt_bias, gdn_out_norm, sc_conv, gla_gate_up, gla_gate_bias, gla_out_norm, diff_q_norm, diff_k_norm, diff_lambda, diff_out_norm, router_group, router_group_bias, router_expert, router_expert_bias, expert_w1, expert_w3, expert_w2):
    bsz, S, _ = x.shape
    n_ctx = ctx.shape[1]
    rows = S // GRID_W
    pos = jnp.arange(rows * GRID_W, dtype=jnp.int32)
    row = (pos // GRID_W).astype(F32)
    col = (pos % GRID_W).astype(F32)
    inv_freq = ROPE_BASE ** (-jnp.arange(ROPE_PAIRS, dtype=F32) / ROPE_PAIRS)
    ang_r, ang_c = row[:, None] * inv_freq, col[:, None] * inv_freq
    rope = (jnp.cos(ang_r), jnp.sin(ang_r), jnp.cos(ang_c), jnp.sin(ang_c))

    xl = x[0]
    xc = ctx[0]
    cond, cond_ctx = jax.nn.silu(c), jax.nn.silu(c_ctx)
    for l in range(DEPTH):
        with_ctx_out = l < DEPTH - 1
        lam_init = 0.8 - 0.6 * math.exp(-0.3 * l)
        m = jnp.split(cond @ w_mod[l] + b_mod[l], 6, axis=-1)
        mc = jnp.split((cond_ctx @ w_mod[l] + b_mod[l])[None, :], 6, axis=-1)

        w_main, w_small = _prep_w_in(w_in[l])
        w_out_b = w_out[l].astype(BF16)
        g1 = norm1[l][None, :]
        p_lat = _split_main(*inproj(xl, g1, m[0], m[1], w_main, w_small))
        p_ctx = _split_main(*inproj(xc, g1, mc[0], mc[1], w_main, w_small))
        a_lat, a_ctx = gdn_mixer(p_lat, p_ctx, gdn_conv[l], gdn_a_log[l], gdn_dt_bias[l], gdn_out_norm[l], with_ctx_out)
        b_lat = short_conv_mixer(p_lat, sc_conv[l])
        g_lat, g_ctx = gla_mixer(p_lat, p_ctx, gla_gate_up[l], gla_gate_bias[l], gla_out_norm[l], with_ctx_out)
        d_lat, d_ctx = diff_mixer(p_lat, p_ctx, rope, diff_q_norm[l], diff_k_norm[l], diff_lambda[l],
                                  diff_out_norm[l], lam_init, with_ctx_out)
        xl = outproj(jnp.concatenate([a_lat, b_lat, g_lat, d_lat], axis=-1)[0], xl, m[2], w_out_b)

        moe_w = (router_group[l], router_group_bias[l], router_expert[l], router_expert_bias[l],
                 expert_w1[l], expert_w3[l], expert_w2[l])
        h2_lat = rms_norm(xl, norm2[l]) * (1.0 + m[4]) + m[3]
        if with_ctx_out:
            b_ctx = short_conv_mixer(p_ctx, sc_conv[l])
            xc = outproj(jnp.concatenate([a_ctx, b_ctx, g_ctx, d_ctx], axis=-1)[0], xc, mc[2], w_out_b)
            h2_ctx = rms_norm(xc, norm2[l]) * (1.0 + mc[4]) + mc[3]
            y = hier_moe(jnp.concatenate([h2_ctx, h2_lat], axis=0)[None], *moe_w)[0]
            xc = xc + mc[5] * y[:n_ctx]
            xl = xl + m[5] * y[n_ctx:]
        else:
            xl = xl + m[5] * hier_moe(h2_lat[None], *moe_w)[0]
    return xl[None]
```

```python
import functools
import math

import jax
import jax.numpy as jnp
from jax import lax
from jax.experimental import pallas as pl
from jax.experimental.pallas import tpu as pltpu

D_MODEL = 2048
SEQ = 16384
DEPTH = 2
CTX_LEN = 256
GRID_W = 64
GROUP_WIDTH = 512
GDN_DK = 64
GDN_DV = 64
GDN_HEADS = GROUP_WIDTH // GDN_DV
SCAN_CHUNK = 64
SC_WIDTH = GROUP_WIDTH
GLA_DK = 64
GLA_DV = 128
GLA_HEADS = GROUP_WIDTH // GLA_DV
GLA_GATE_RANK = 16
GLA_TAU = 16.0
DIFF_DK = 64
DIFF_DV = 128
DIFF_HEADS = GROUP_WIDTH // DIFF_DV
Q_BLOCK = 128
ROPE_BASE = 10000.0
ROPE_PAIRS = DIFF_DK // 4
N_GROUPS = 4
EXPERTS_PER_GROUP = 8
N_EXPERTS = N_GROUPS * EXPERTS_PER_GROUP
TOP_K = 2
D_EXPERT = D_MODEL // 4
EXPERT_BLOCK = 256
NORM_EPS = 1e-6
F32 = jnp.float32
BF16 = jnp.bfloat16

PROJ_LAYOUT = (
    ('gdn_qkv', GDN_HEADS * (2 * GDN_DK + GDN_DV)),
    ('gdn_gate', GDN_HEADS * GDN_DV),
    ('gdn_beta', 2 * GDN_HEADS),
    ('gdn_alpha', 2 * GDN_HEADS),
    ('sc_b', SC_WIDTH),
    ('sc_c', SC_WIDTH),
    ('sc_x', SC_WIDTH),
    ('gla_q', GLA_HEADS * GLA_DK),
    ('gla_k', GLA_HEADS * GLA_DK),
    ('gla_v', GLA_HEADS * GLA_DV),
    ('gla_r', GLA_HEADS * GLA_DV),
    ('gla_lowrank', 2 * GLA_GATE_RANK),
    ('diff_q', DIFF_HEADS * 2 * DIFF_DK),
    ('diff_k', DIFF_HEADS * 2 * DIFF_DK),
    ('diff_v', DIFF_HEADS * DIFF_DV),
)
PROJ_OFFSETS = {}
_off = 0
for _name, _size in PROJ_LAYOUT:
    PROJ_OFFSETS[_name] = (_off, _size)
    _off += _size
PROJ_WIDTH = _off

MAIN_GROUPS = ('gdn_qkv', 'gdn_gate', 'sc_b', 'sc_c', 'sc_x', 'gla_q', 'gla_k', 'gla_v', 'gla_r',
               'diff_q', 'diff_k', 'diff_v')
SMALL_GROUPS = ('gdn_beta', 'gdn_alpha', 'gla_lowrank')
MAIN_OFFSETS = {}
_off = 0
for _name in MAIN_GROUPS:
    MAIN_OFFSETS[_name] = (_off, PROJ_OFFSETS[_name][1])
    _off += PROJ_OFFSETS[_name][1]
MAIN_WIDTH = _off
SMALL_OFFSETS = {}
_off = 0
for _name in SMALL_GROUPS:
    SMALL_OFFSETS[_name] = (_off, PROJ_OFFSETS[_name][1])
    _off += PROJ_OFFSETS[_name][1]
LANE = 128
SMALL_WIDTH = LANE

VMEM_LIMIT = 48 * 1024 * 1024


INPROJ_TM = 1024
INPROJ_TN = 512


def _inproj_body(x_ref, gain_ref, shift_ref, scale_ref, w_ref, ws_ref, o_ref, os_ref, h_ref):
    @pl.when(pl.program_id(1) == 0)
    def _():
        x = x_ref[...]
        y = x * lax.rsqrt(jnp.mean(x * x, axis=-1, keepdims=True) + NORM_EPS) * gain_ref[...]
        hb = (y * (1.0 + scale_ref[...]) + shift_ref[...]).astype(BF16)
        h_ref[...] = hb
        os_ref[...] = jnp.dot(hb, ws_ref[...], preferred_element_type=F32)

    o_ref[...] = jnp.dot(h_ref[...], w_ref[...], preferred_element_type=F32)


def inproj(x, gain, shift, scale, w_main, w_small):
    rows = x.shape[0]
    tm = min(INPROJ_TM, rows)
    grid = (pl.cdiv(rows, tm), MAIN_WIDTH // INPROJ_TN)
    vec = pl.BlockSpec((1, D_MODEL), lambda i, j: (0, 0))
    return pl.pallas_call(
        _inproj_body,
        grid=grid,
        in_specs=[
            pl.BlockSpec((tm, D_MODEL), lambda i, j: (i, 0)),
            vec, vec, vec,
            pl.BlockSpec((D_MODEL, INPROJ_TN), lambda i, j: (0, j)),
            pl.BlockSpec((D_MODEL, SMALL_WIDTH), lambda i, j: (0, 0)),
        ],
        out_specs=[
            pl.BlockSpec((tm, INPROJ_TN), lambda i, j: (i, j)),
            pl.BlockSpec((tm, SMALL_WIDTH), lambda i, j: (i, 0)),
        ],
        out_shape=[
            jax.ShapeDtypeStruct((rows, MAIN_WIDTH), F32),
            jax.ShapeDtypeStruct((rows, SMALL_WIDTH), F32),
        ],
        scratch_shapes=[pltpu.VMEM((tm, D_MODEL), BF16)],
        compiler_params=pltpu.CompilerParams(
            dimension_semantics=("parallel", "arbitrary"), vmem_limit_bytes=VMEM_LIMIT),
        name="inproj",
    )(x, gain, shift, scale, w_main, w_small)


OUTPROJ_TM = 512


def _outproj_body(a_ref, x_ref, gate_ref, w_ref, o_ref):
    y = jnp.dot(a_ref[...].astype(BF16), w_ref[...], preferred_element_type=F32)
    o_ref[...] = x_ref[...] + gate_ref[...] * y


def outproj(a, x, gate, w):
    rows = x.shape[0]
    tm = min(OUTPROJ_TM, rows)
    return pl.pallas_call(
        _outproj_body,
        grid=(pl.cdiv(rows, tm),),
        in_specs=[
            pl.BlockSpec((tm, D_MODEL), lambda i: (i, 0)),
            pl.BlockSpec((tm, D_MODEL), lambda i: (i, 0)),
            pl.BlockSpec((1, D_MODEL), lambda i: (0, 0)),
            pl.BlockSpec((D_MODEL, D_MODEL), lambda i: (0, 0)),
        ],
        out_specs=pl.BlockSpec((tm, D_MODEL), lambda i: (i, 0)),
        out_shape=jax.ShapeDtypeStruct((rows, D_MODEL), F32),
        compiler_params=pltpu.CompilerParams(
            dimension_semantics=("parallel",), vmem_limit_bytes=VMEM_LIMIT),
        name="outproj",
    )(a, x, gate, w)


HEAD_LANES = 2 * DIFF_DK
ATTN_TK = 512
ATTN_TQ = 512
ATTN_UNROLL = 1
NEG_BIG = -1e30


def _group_mean_sq(x, width):
    n = x.shape[-1]
    gi = lax.broadcasted_iota(jnp.int32, (n, n), 0) // width
    gj = lax.broadcasted_iota(jnp.int32, (n, n), 1) // width
    ones = jnp.where(gi == gj, 1.0, 0.0).astype(F32)
    return jnp.dot(x * x, ones, preferred_element_type=F32, precision=lax.Precision.HIGHEST) * (1.0 / width)


def _rope_swap(x):
    n = x.shape[-1]
    lane = lax.broadcasted_iota(jnp.int32, x.shape, x.ndim - 1)
    return jnp.where((lane % (2 * ROPE_PAIRS)) < ROPE_PAIRS,
                     pltpu.roll(x, n - ROPE_PAIRS, 1), pltpu.roll(x, ROPE_PAIRS, 1))


def _attn_prep_body(q_ref, k_ref, v_ref, cos_ref, sin_ref, qg_ref, kg_ref, qo_ref, ko_ref, vt_ref):
    cos = jnp.concatenate([cos_ref[...]] * DIFF_HEADS, axis=-1)
    sin = jnp.concatenate([sin_ref[...]] * DIFF_HEADS, axis=-1)

    def norm_rope(x, gain):
        y = x * lax.rsqrt(_group_mean_sq(x, DIFF_DK) + NORM_EPS) * gain
        return y * cos + _rope_swap(y) * sin

    q = norm_rope(q_ref[...], qg_ref[...]) * (DIFF_DK ** -0.5)
    lane = lax.broadcasted_iota(jnp.int32, q.shape, 1)
    first = (lane % HEAD_LANES) < DIFF_DK
    qo_ref[0] = jnp.where(first, q, 0.0).astype(BF16)
    qo_ref[1] = jnp.where(first, 0.0, q).astype(BF16)
    ko_ref[...] = norm_rope(k_ref[...], kg_ref[...]).astype(BF16)
    v = v_ref[...]
    for h in range(DIFF_HEADS):
        vt_ref[h, 0] = v[:, h * DIFF_DV:(h + 1) * DIFF_DV].T.astype(BF16)


def attn_prep(p_main, cos, sin, q_gain, k_gain):
    rows = p_main.shape[0]
    tm = min(ATTN_TK, rows)
    nb = rows // tm
    col = lambda name: MAIN_OFFSETS[name][0] // GROUP_WIDTH
    seg = lambda name: pl.BlockSpec((tm, GROUP_WIDTH), lambda i, c=col(name): (i, c))
    tab = pl.BlockSpec((tm, HEAD_LANES), lambda i: (i, 0))
    vec = pl.BlockSpec((1, GROUP_WIDTH), lambda i: (0, 0))
    return pl.pallas_call(
        _attn_prep_body,
        grid=(nb,),
        in_specs=[seg('diff_q'), seg('diff_k'), seg('diff_v'), tab, tab, vec, vec],
        out_specs=[
            pl.BlockSpec((2, tm, GROUP_WIDTH), lambda i: (0, i, 0)),
            pl.BlockSpec((tm, GROUP_WIDTH), lambda i: (i, 0)),
            pl.BlockSpec((DIFF_HEADS, 1, DIFF_DV, tm), lambda i: (0, i, 0, 0)),
        ],
        out_shape=[
            jax.ShapeDtypeStruct((2, rows, GROUP_WIDTH), BF16),
            jax.ShapeDtypeStruct((rows, GROUP_WIDTH), BF16),
            jax.ShapeDtypeStruct((DIFF_HEADS, nb, DIFF_DV, tm), BF16),
        ],
        compiler_params=pltpu.CompilerParams(dimension_semantics=("parallel",), vmem_limit_bytes=VMEM_LIMIT),
        name="attn_prep",
    )(p_main, p_main, p_main, cos, sin, q_gain, k_gain)


def _attn_body(lam_ref, q_ref, *rest, n_lat_chunks, tq, post_scale):
    if n_lat_chunks:
        kl_ref, vtl_ref, kc_ref, vtc_ref, gn_ref, o_ref, m_ref, l_ref, acc_ref = rest
    else:
        kc_ref, vtc_ref, gn_ref, o_ref, m_ref, l_ref, acc_ref = rest
    q2 = q_ref[...].reshape(2 * tq, HEAD_LANES)
    m_ref[...] = jnp.full(m_ref.shape, NEG_BIG, F32)
    l_ref[...] = jnp.zeros(l_ref.shape, F32)
    acc_ref[...] = jnp.zeros(acc_ref.shape, F32)

    def step(kc, vt):
        st = lax.dot_general(kc, q2, (((1,), (1,)), ((), ())), preferred_element_type=F32)
        m_prev = m_ref[...]
        m_new = jnp.maximum(m_prev, jnp.max(st, axis=0, keepdims=True))
        alpha = jnp.exp(m_prev - m_new)
        p = jnp.exp(st - m_new)
        l_ref[...] = alpha * l_ref[...] + jnp.sum(p, axis=0, keepdims=True)
        acc_ref[...] = alpha * acc_ref[...] + jnp.dot(vt, p.astype(BF16), preferred_element_type=F32)
        m_ref[...] = m_new

    if n_lat_chunks:
        def body(j, carry):
            step(kl_ref[pl.ds(pl.multiple_of(j * ATTN_TK, ATTN_TK), ATTN_TK), :], vtl_ref[j])
            return carry
        lax.fori_loop(0, n_lat_chunks, body, 0, unroll=ATTN_UNROLL)
    step(kc_ref[...], vtc_ref[0])

    acc = acc_ref[...]
    l = l_ref[...]
    o = acc[:, :tq] / l[:, :tq] - lam_ref[0, 0] * (acc[:, tq:] / l[:, tq:])
    y = o * lax.rsqrt(jnp.mean(o * o, axis=0, keepdims=True) + NORM_EPS) * (gn_ref[...] * post_scale)
    o_ref[...] = y.T


def diff_attention(lam, q, k_lat, vt_lat, k_ctx, vt_ctx, out_gain, post_scale):
    rows = q.shape[1]
    tq = min(ATTN_TQ, rows)
    n_lat_chunks = 0 if k_lat is None else k_lat.shape[0] // ATTN_TK
    n_ctx = k_ctx.shape[0]
    in_specs = [
        pl.BlockSpec(memory_space=pltpu.SMEM),
        pl.BlockSpec((2, tq, HEAD_LANES), lambda h, i: (0, i, h)),
    ]
    args = [lam, q]
    if n_lat_chunks:
        in_specs += [
            pl.BlockSpec((k_lat.shape[0], HEAD_LANES), lambda h, i: (0, h)),
            pl.BlockSpec((None, n_lat_chunks, DIFF_DV, ATTN_TK), lambda h, i: (h, 0, 0, 0)),
        ]
        args += [k_lat, vt_lat]
    in_specs += [
        pl.BlockSpec((n_ctx, HEAD_LANES), lambda h, i: (0, h)),
        pl.BlockSpec((None, 1, DIFF_DV, n_ctx), lambda h, i: (h, 0, 0, 0)),
        pl.BlockSpec((DIFF_DV, 1), lambda h, i: (0, 0)),
    ]
    args += [k_ctx, vt_ctx, out_gain]
    return pl.pallas_call(
        functools.partial(_attn_body, n_lat_chunks=n_lat_chunks, tq=tq, post_scale=post_scale),
        grid=(DIFF_HEADS, rows // tq),
        in_specs=in_specs,
        out_specs=pl.BlockSpec((tq, DIFF_DV), lambda h, i: (i, h)),
        out_shape=jax.ShapeDtypeStruct((rows, GROUP_WIDTH), F32),
        scratch_shapes=[
            pltpu.VMEM((1, 2 * tq), F32),
            pltpu.VMEM((1, 2 * tq), F32),
            pltpu.VMEM((DIFF_DV, 2 * tq), F32),
        ],
        compiler_params=pltpu.CompilerParams(
            dimension_semantics=("parallel", "arbitrary"), vmem_limit_bytes=VMEM_LIMIT),
        name="diff_attn_lat" if n_lat_chunks else "diff_attn_ctx",
    )(*args)


def rope_tables(n_lat, n_ctx):
    pos = jnp.arange(n_lat, dtype=jnp.int32)
    row = (pos // GRID_W).astype(F32)
    col = (pos % GRID_W).astype(F32)
    inv_freq = ROPE_BASE ** (-jnp.arange(ROPE_PAIRS, dtype=F32) / ROPE_PAIRS)
    ang_r, ang_c = row[:, None] * inv_freq, col[:, None] * inv_freq
    cos = jnp.concatenate([jnp.cos(ang_r), jnp.cos(ang_r), jnp.cos(ang_c), jnp.cos(ang_c)], axis=-1)
    sin = jnp.concatenate([-jnp.sin(ang_r), jnp.sin(ang_r), -jnp.sin(ang_c), jnp.sin(ang_c)], axis=-1)
    cos, sin = jnp.tile(cos, (1, 2)), jnp.tile(sin, (1, 2))
    return cos, sin, jnp.ones((n_ctx, HEAD_LANES), F32), jnp.zeros((n_ctx, HEAD_LANES), F32)


def diff_lambda_scalar(lam_params, lam_init):
    lq1, lk1, lq2, lk2 = lam_params
    return (jnp.exp(jnp.sum(lq1 * lk1)) - jnp.exp(jnp.sum(lq2 * lk2)) + lam_init).reshape(1, 1)


def _ffn_body(be_ref, nu_ref, x_ref, w1_ref, w3_ref, w2_ref, o_ref, w1b_ref, w3b_ref, w2b_ref):
    b = pl.program_id(0)
    used = b < nu_ref[0]
    new_expert = jnp.logical_or(b == 0, be_ref[b] != be_ref[jnp.maximum(b - 1, 0)])

    @pl.when(jnp.logical_and(used, new_expert))
    def _():
        w1b_ref[...] = w1_ref[...].astype(BF16)
        w3b_ref[...] = w3_ref[...].astype(BF16)
        w2b_ref[...] = w2_ref[...].astype(BF16)

    @pl.when(used)
    def _():
        x = x_ref[...]
        a = jnp.dot(x, w1b_ref[...], preferred_element_type=F32)
        g = jnp.dot(x, w3b_ref[...], preferred_element_type=F32)
        mid = (a * jax.nn.sigmoid(a) * g).astype(BF16)
        o_ref[...] = jnp.dot(mid, w2b_ref[...], preferred_element_type=F32)

    @pl.when(jnp.logical_not(used))
    def _():
        o_ref[...] = jnp.zeros(o_ref.shape, F32)


def expert_ffn(block_expert, n_used, xs, w1, w3, w2):
    L = xs.shape[0]
    n_blocks = L // EXPERT_BLOCK
    wspec = lambda shape: pl.BlockSpec((None,) + shape, lambda b, be, nu: (be[b], 0, 0))
    return pl.pallas_call(
        _ffn_body,
        grid_spec=pltpu.PrefetchScalarGridSpec(
            num_scalar_prefetch=2,
            grid=(n_blocks,),
            in_specs=[
                pl.BlockSpec((EXPERT_BLOCK, D_MODEL), lambda b, be, nu: (b, 0)),
                wspec((D_MODEL, D_EXPERT)), wspec((D_MODEL, D_EXPERT)), wspec((D_EXPERT, D_MODEL)),
            ],
            out_specs=pl.BlockSpec((EXPERT_BLOCK, D_MODEL), lambda b, be, nu: (b, 0)),
            scratch_shapes=[
                pltpu.VMEM((D_MODEL, D_EXPERT), BF16),
                pltpu.VMEM((D_MODEL, D_EXPERT), BF16),
                pltpu.VMEM((D_EXPERT, D_MODEL), BF16),
            ],
        ),
        out_shape=jax.ShapeDtypeStruct((L, D_MODEL), F32),
        compiler_params=pltpu.CompilerParams(dimension_semantics=("arbitrary",), vmem_limit_bytes=VMEM_LIMIT),
        name="expert_ffn",
    )(block_expert, n_used, xs, w1, w3, w2)


def hier_moe_pallas(hf, wg, bg, we, be, w1, w3, w2):
    N, D = hf.shape
    g_prob = jax.nn.softmax(jnp.dot(hf, wg, precision=lax.Precision.HIGHEST) + bg, axis=-1)
    g_top_p, g_idx = lax.top_k(g_prob, 1)
    e_logits = (jnp.dot(hf, we, precision=lax.Precision.HIGHEST) + be).reshape(N, N_GROUPS, EXPERTS_PER_GROUP)
    e_in_group = jnp.take_along_axis(e_logits, g_idx[:, :, None], axis=1)[:, 0]
    e_top_p, e_top_i = lax.top_k(jax.nn.softmax(e_in_group, axis=-1), TOP_K)
    e_top_p = e_top_p / jnp.sum(e_top_p, axis=-1, keepdims=True)
    weights = g_top_p * e_top_p
    expert_idx = g_idx * EXPERTS_PER_GROUP + e_top_i

    E, M, K = N_EXPERTS, EXPERT_BLOCK, TOP_K
    A = N * K
    flat_e = expert_idx.reshape(-1).astype(jnp.int32)
    order = jnp.argsort(flat_e)
    e_s = flat_e[order]
    t_s = (order // K).astype(jnp.int32)
    counts = jnp.bincount(flat_e, length=E).astype(jnp.int32)
    padded = (counts + M - 1) // M * M
    pad_end = jnp.cumsum(padded)
    pad_start = pad_end - padded
    start = jnp.cumsum(counts) - counts
    dest = pad_start[e_s] + (jnp.arange(A, dtype=jnp.int32) - start[e_s])
    n_blocks = -(-A // M) + E
    L = n_blocks * M
    slot_tok = jnp.full((L,), N, jnp.int32).at[dest].set(t_s)
    block_expert = jnp.minimum(
        jnp.searchsorted(pad_end, jnp.arange(n_blocks, dtype=jnp.int32) * M, side='right'), E - 1).astype(jnp.int32)
    n_used = (pad_end[-1] // M).astype(jnp.int32).reshape(1)
    h_pad = jnp.concatenate([hf.astype(BF16), jnp.zeros((1, D), BF16)], axis=0)
    xs = h_pad[slot_tok]
    y = expert_ffn(block_expert, n_used, xs, w1, w3, w2)
    pos = jnp.zeros((A,), jnp.int32).at[order].set(dest).reshape(N, K)
    return y[pos[:, 0]] * weights[:, 0:1] + y[pos[:, 1]] * weights[:, 1:2]


def rms_norm(x, gain):
    y = x * lax.rsqrt(jnp.mean(x * x, axis=-1, keepdims=True) + NORM_EPS)
    return y * gain


def l2_normalize(x):
    return x * lax.rsqrt(jnp.sum(x * x, axis=-1, keepdims=True) + NORM_EPS)


def dwconv_centred(x, w):
    pad = w.shape[0] // 2
    return lax.conv_general_dilated(
        x, w[:, None, :], window_strides=(1,), padding=((pad, pad),),
        dimension_numbers=('NWC', 'WIO', 'NWC'), feature_group_count=x.shape[-1])


def to_chunks(t, chunk):
    bsz, T, H = t.shape[:3]
    t = t.reshape((bsz, T // chunk, chunk, H) + t.shape[3:])
    return jnp.moveaxis(t, (1, 3), (0, 2))


def from_chunks(t):
    t = jnp.moveaxis(t, (0, 2), (1, 3))
    return t.reshape((t.shape[0], t.shape[1] * t.shape[2]) + t.shape[3:])


def gated_delta_chunked(q, k, v, log_decay, beta, state0):
    C = SCAN_CHUNK
    q = to_chunks(q, C) * (q.shape[-1] ** -0.5)
    k = to_chunks(k, C)
    v = to_chunks(v, C)
    beta = to_chunks(beta, C)
    g = jnp.cumsum(to_chunks(log_decay, C), axis=-1)
    incl = jnp.tril(jnp.ones((C, C), bool))
    strict = jnp.tril(jnp.ones((C, C), bool), -1)
    decay = jnp.exp(jnp.where(incl, g[..., :, None] - g[..., None, :], -jnp.inf))
    kb = k * beta[..., None]
    lower = jnp.where(strict, jnp.einsum('nbhid,nbhjd->nbhij', kb, k) * decay, 0.0)
    tmat = lower + jnp.eye(C, dtype=F32)
    solve = functools.partial(lax.linalg.triangular_solve, left_side=True, lower=True, unit_diagonal=True)
    u = solve(tmat, v * beta[..., None])
    w = solve(tmat, kb * jnp.exp(g)[..., None])
    attn = jnp.einsum('nbhid,nbhjd->nbhij', q, k) * decay
    g_last = g[..., -1]
    q_dec = q * jnp.exp(g)[..., None]
    k_dec = k * jnp.exp(g_last[..., None] - g)[..., None]

    def step(S, xs):
        u_n, w_n, q_n, k_n, a_n, gl_n = xs
        v_new = u_n - jnp.einsum('bhcd,bhde->bhce', w_n, S)
        o_n = jnp.einsum('bhcd,bhde->bhce', q_n, S) + jnp.einsum('bhij,bhje->bhie', a_n, v_new)
        S = S * jnp.exp(gl_n)[..., None, None] + jnp.einsum('bhcd,bhce->bhde', k_n, v_new)
        return S, o_n

    S, o = lax.scan(step, state0, (u, w, q_dec, k_dec, attn, g_last))
    return from_chunks(o), S


def gla_chunked(q, k, v, log_a, state0):
    C = SCAN_CHUNK
    q = to_chunks(q, C) * (q.shape[-1] ** -0.5)
    k = to_chunks(k, C)
    v = to_chunks(v, C)
    b = jnp.cumsum(to_chunks(log_a, C), axis=-2)
    b_last = b[..., -1:, :]
    k_out = k * jnp.exp(b_last - b)
    incl = jnp.tril(jnp.ones((C, C), bool))
    attn = jnp.where(incl, jnp.einsum('nbhid,nbhjd->nbhij', q * jnp.exp(b - b_last), k_out), 0.0)
    contrib = jnp.einsum('nbhcd,nbhce->nbhde', k_out, v)

    def step(S, xs):
        a, dS = xs
        return a[..., None] * S + dS, S

    S, S_start = lax.scan(step, state0, (jnp.exp(b_last[..., 0, :]), contrib))
    o = jnp.einsum('nbhij,nbhje->nbhie', attn, v) + jnp.einsum('nbhcd,nbhde->nbhce', q * jnp.exp(b), S_start)
    return from_chunks(o), S


def bidir_prefix_scan(scan_fn, ctx_dirs, lat_dirs, state0):
    o_ctx, o_lat = 0.0, 0.0
    for d in range(2):
        rev = (lambda t: jnp.flip(t, axis=1)) if d == 1 else (lambda t: t)
        oc, s_ctx = scan_fn(*[rev(t) for t in ctx_dirs[d]], state0)
        ol, _ = scan_fn(*[rev(t) for t in lat_dirs[d]], s_ctx)
        o_ctx = o_ctx + rev(oc)
        o_lat = o_lat + rev(ol)
    return o_ctx, o_lat


def gdn_mixer(pl_, pc, conv_w, a_log, dt_bias, out_norm, with_ctx_out):
    hk = GDN_HEADS * GDN_DK

    def prep(pp):
        bsz, T = pp['gdn_qkv'].shape[:2]
        qkv = jax.nn.silu(dwconv_centred(pp['gdn_qkv'], conv_w))
        q, k, v = jnp.split(qkv, [hk, 2 * hk], axis=-1)
        q = l2_normalize(q.reshape(bsz, T, GDN_HEADS, GDN_DK))
        k = l2_normalize(k.reshape(bsz, T, GDN_HEADS, GDN_DK))
        v = v.reshape(bsz, T, GDN_HEADS, GDN_DV)
        beta = jax.nn.sigmoid(pp['gdn_beta']).reshape(bsz, T, 2, GDN_HEADS)
        alpha = pp['gdn_alpha'].reshape(bsz, T, 2, GDN_HEADS)
        log_decay = -jnp.exp(a_log) * jax.nn.softplus(alpha + dt_bias)
        return [(q, k, v, log_decay[:, :, d], beta[:, :, d]) for d in range(2)]

    bsz = pl_['gdn_qkv'].shape[0]
    s0 = jnp.zeros((bsz, GDN_HEADS, GDN_DK, GDN_DV), F32)
    o_ctx, o_lat = bidir_prefix_scan(gated_delta_chunked, prep(pc), prep(pl_), s0)

    def post(o, pp):
        bsz, T = o.shape[:2]
        gate = pp['gdn_gate'].reshape(bsz, T, GDN_HEADS, GDN_DV)
        y = rms_norm(o, out_norm) * jax.nn.silu(gate)
        return y.reshape(bsz, T, GDN_HEADS * GDN_DV)

    return post(o_lat, pl_), (post(o_ctx, pc) if with_ctx_out else None)


def short_conv_mixer(pp, conv_w):
    return pp['sc_b'] * dwconv_centred(pp['sc_c'] * pp['sc_x'], conv_w)


def gla_mixer(pl_, pc, gate_up, gate_bias, out_norm, with_ctx_out):
    def prep(pp):
        bsz, T = pp['gla_q'].shape[:2]
        q = pp['gla_q'].reshape(bsz, T, GLA_HEADS, GLA_DK)
        k = pp['gla_k'].reshape(bsz, T, GLA_HEADS, GLA_DK)
        v = pp['gla_v'].reshape(bsz, T, GLA_HEADS, GLA_DV)
        lowrank = pp['gla_lowrank'].reshape(bsz, T, 2, GLA_GATE_RANK)
        z = jnp.einsum('btnr,nrk->btnk', lowrank, gate_up) + gate_bias
        log_a = (jax.nn.log_sigmoid(z) / GLA_TAU).reshape(bsz, T, 2, GLA_HEADS, GLA_DK)
        return [(q, k, v, log_a[:, :, d]) for d in range(2)]

    bsz = pl_['gla_q'].shape[0]
    s0 = jnp.zeros((bsz, GLA_HEADS, GLA_DK, GLA_DV), F32)
    o_ctx, o_lat = bidir_prefix_scan(gla_chunked, prep(pc), prep(pl_), s0)

    def post(o, pp):
        bsz, T = o.shape[:2]
        r = pp['gla_r'].reshape(bsz, T, GLA_HEADS, GLA_DV)
        y = rms_norm(o, out_norm) * jax.nn.silu(r)
        return y.reshape(bsz, T, GLA_HEADS * GLA_DV)

    return post(o_lat, pl_), (post(o_ctx, pc) if with_ctx_out else None)


def axial_rope(x, cos_r, sin_r, cos_c, sin_c):
    r1, r2, c1, c2 = jnp.split(x, 4, axis=-1)
    ex = lambda t: t[None, :, None, None, :]
    cr, sr, cc, sc = ex(cos_r), ex(sin_r), ex(cos_c), ex(sin_c)
    return jnp.concatenate([r1 * cr - r2 * sr, r2 * cr + r1 * sr, c1 * cc - c2 * sc, c2 * cc + c1 * sc], axis=-1)


def diff_softmax_attend(q, k, v, lam):
    s = jnp.einsum('bqhcd,bkhcd->bhcqk', q, k) * (q.shape[-1] ** -0.5)
    p = jax.nn.softmax(s, axis=-1)
    w = p[:, :, 0] - lam * p[:, :, 1]
    return jnp.einsum('bhqk,bkhd->bqhd', w, v)


def diff_mixer(pl_, pc, rope, q_norm, k_norm, lam_params, out_norm, lam_init, with_ctx_out):
    bsz, S = pl_['diff_q'].shape[:2]
    n_ctx = pc['diff_k'].shape[1]
    qk_shape = lambda T: (bsz, T, DIFF_HEADS, 2, DIFF_DK)
    q_lat = axial_rope(rms_norm(pl_['diff_q'].reshape(qk_shape(S)), q_norm), *rope)
    k_lat = axial_rope(rms_norm(pl_['diff_k'].reshape(qk_shape(S)), k_norm), *rope)
    k_ctx = rms_norm(pc['diff_k'].reshape(qk_shape(n_ctx)), k_norm)
    v_lat = pl_['diff_v'].reshape(bsz, S, DIFF_HEADS, DIFF_DV)
    v_ctx = pc['diff_v'].reshape(bsz, n_ctx, DIFF_HEADS, DIFF_DV)
    lq1, lk1, lq2, lk2 = lam_params
    lam = jnp.exp(jnp.sum(lq1 * lk1)) - jnp.exp(jnp.sum(lq2 * lk2)) + lam_init
    k_all = jnp.concatenate([k_ctx, k_lat], axis=1)
    v_all = jnp.concatenate([v_ctx, v_lat], axis=1)
    q_blocks = jnp.moveaxis(q_lat.reshape(bsz, S // Q_BLOCK, Q_BLOCK, DIFF_HEADS, 2, DIFF_DK), 1, 0)
    o_lat = lax.map(lambda qb: diff_softmax_attend(qb, k_all, v_all, lam), q_blocks)
    o_lat = jnp.moveaxis(o_lat, 0, 1).reshape(bsz, S, DIFF_HEADS, DIFF_DV)

    def post(o):
        y = rms_norm(o, out_norm) * (1.0 - lam_init)
        return y.reshape(o.shape[0], o.shape[1], DIFF_HEADS * DIFF_DV)

    if not with_ctx_out:
        return post(o_lat), None
    q_ctx = rms_norm(pc['diff_q'].reshape(qk_shape(n_ctx)), q_norm)
    return post(o_lat), post(diff_softmax_attend(q_ctx, k_ctx, v_ctx, lam))


def grouped_expert_ffn(h, expert_idx, weights, w1, w3, w2):
    N, D = h.shape
    K = expert_idx.shape[1]
    E = w1.shape[0]
    M = EXPERT_BLOCK
    A = N * K
    flat_e = expert_idx.reshape(-1)
    flat_t = jnp.arange(A, dtype=jnp.int32) // K
    flat_w = weights.reshape(-1)
    order = jnp.argsort(flat_e)
    e_s, t_s, w_s = flat_e[order], flat_t[order], flat_w[order]
    counts = jnp.bincount(flat_e, length=E)
    padded = (counts + M - 1) // M * M
    pad_end = jnp.cumsum(padded)
    pad_start = pad_end - padded
    start = jnp.cumsum(counts) - counts
    dest = pad_start[e_s] + (jnp.arange(A, dtype=jnp.int32) - start[e_s])
    n_blocks = -(-A // M) + E
    L = n_blocks * M
    slot_tok = jnp.full((L,), N, jnp.int32).at[dest].set(t_s)
    slot_w = jnp.zeros((L,), h.dtype).at[dest].set(w_s.astype(h.dtype))
    block_expert = jnp.minimum(jnp.searchsorted(pad_end, jnp.arange(n_blocks, dtype=jnp.int32) * M, side='right'), E - 1)
    h_pad = jnp.concatenate([h, jnp.zeros((1, D), h.dtype)], axis=0)

    def run_block(args):
        tok, e = args
        xb = h_pad[tok]
        return (jax.nn.silu(xb @ w1[e]) * (xb @ w3[e])) @ w2[e]

    y = lax.map(run_block, (slot_tok.reshape(n_blocks, M), block_expert))
    y = y.reshape(L, D) * slot_w[:, None]
    return jnp.zeros((N + 1, D), h.dtype).at[slot_tok].add(y)[:N]


def hier_moe(h, wg, bg, we, be, w1, w3, w2):
    bsz, T, D = h.shape
    hf = h.reshape(-1, D)
    N = hf.shape[0]
    g_prob = jax.nn.softmax(hf @ wg + bg, axis=-1)
    g_top_p, g_idx = lax.top_k(g_prob, 1)
    e_logits = (hf @ we + be).reshape(N, N_GROUPS, EXPERTS_PER_GROUP)
    e_in_group = e_logits[jnp.arange(N), g_idx[:, 0]]
    e_top_p, e_top_i = lax.top_k(jax.nn.softmax(e_in_group, axis=-1), TOP_K)
    e_top_p = e_top_p / jnp.sum(e_top_p, axis=-1, keepdims=True)
    weights = g_top_p * e_top_p
    expert_idx = g_idx * EXPERTS_PER_GROUP + e_top_i
    return grouped_expert_ffn(hf, expert_idx, weights, w1, w3, w2).reshape(bsz, T, D)


def _split_main(p_main, p_small):
    parts = {}
    for name, (off, size) in MAIN_OFFSETS.items():
        parts[name] = p_main[None, :, off:off + size]
    for name, (off, size) in SMALL_OFFSETS.items():
        parts[name] = p_small[None, :, off:off + size]
    return parts


def _prep_w_in(w):
    main = jnp.concatenate([w[:, PROJ_OFFSETS[n][0]:PROJ_OFFSETS[n][0] + PROJ_OFFSETS[n][1]] for n in MAIN_GROUPS], axis=1)
    small = jnp.concatenate([w[:, PROJ_OFFSETS[n][0]:PROJ_OFFSETS[n][0] + PROJ_OFFSETS[n][1]] for n in SMALL_GROUPS], axis=1)
    small = jnp.pad(small, ((0, 0), (0, SMALL_WIDTH - small.shape[1])))
    return main.astype(BF16), small.astype(BF16)


def kernel(x, c, ctx, c_ctx, w_mod, b_mod, norm1, norm2, w_in, w_out, gdn_conv, gdn_a_log, gdn_dt_bias, gdn_out_norm, sc_conv, gla_gate_up, gla_gate_bias, gla_out_norm, diff_q_norm, diff_k_norm, diff_lambda, diff_out_norm, router_group, router_group_bias, router_expert, router_expert_bias, expert_w1, expert_w3, expert_w2):
    bsz, S, _ = x.shape
    n_ctx = ctx.shape[1]
    cos_l, sin_l, cos_c, sin_c = rope_tables(S, n_ctx)

    xl = x[0]
    xc = ctx[0]
    cond, cond_ctx = jax.nn.silu(c), jax.nn.silu(c_ctx)
    for l in range(DEPTH):
        with_ctx_out = l < DEPTH - 1
        lam_init = 0.8 - 0.6 * math.exp(-0.3 * l)
        m = jnp.split(cond @ w_mod[l] + b_mod[l], 6, axis=-1)
        mc = jnp.split((cond_ctx @ w_mod[l] + b_mod[l])[None, :], 6, axis=-1)

        w_main, w_small = _prep_w_in(w_in[l])
        w_out_b = w_out[l].astype(BF16)
        g1 = norm1[l][None, :]
        pm_lat, ps_lat = inproj(xl, g1, m[0], m[1], w_main, w_small)
        pm_ctx, ps_ctx = inproj(xc, g1, mc[0], mc[1], w_main, w_small)
        p_lat = _split_main(pm_lat, ps_lat)
        p_ctx = _split_main(pm_ctx, ps_ctx)
        a_lat, a_ctx = gdn_mixer(p_lat, p_ctx, gdn_conv[l], gdn_a_log[l], gdn_dt_bias[l], gdn_out_norm[l], with_ctx_out)
        b_lat = short_conv_mixer(p_lat, sc_conv[l])
        g_lat, g_ctx = gla_mixer(p_lat, p_ctx, gla_gate_up[l], gla_gate_bias[l], gla_out_norm[l], with_ctx_out)

        lam = diff_lambda_scalar(diff_lambda[l], lam_init)
        q_gain = jnp.tile(diff_q_norm[l], 2 * DIFF_HEADS)[None, :]
        k_gain = jnp.tile(diff_k_norm[l], 2 * DIFF_HEADS)[None, :]
        d_gain = diff_out_norm[l][:, None]
        q_l, k_l, vt_l = attn_prep(pm_lat, cos_l, sin_l, q_gain, k_gain)
        q_c, k_c, vt_c = attn_prep(pm_ctx, cos_c, sin_c, q_gain, k_gain)
        d_lat = diff_attention(lam, q_l, k_l, vt_l, k_c, vt_c, d_gain, 1.0 - lam_init)[None]
        xl = outproj(jnp.concatenate([a_lat, b_lat, g_lat, d_lat], axis=-1)[0], xl, m[2], w_out_b)

        moe_w = (router_group[l], router_group_bias[l], router_expert[l], router_expert_bias[l],
                 expert_w1[l], expert_w3[l], expert_w2[l])
        h2_lat = rms_norm(xl, norm2[l]) * (1.0 + m[4]) + m[3]
        if with_ctx_out:
            b_ctx = short_conv_mixer(p_ctx, sc_conv[l])
            d_ctx = diff_attention(lam, q_c, None, None, k_c, vt_c, d_gain, 1.0 - lam_init)[None]
            xc = outproj(jnp.concatenate([a_ctx, b_ctx, g_ctx, d_ctx], axis=-1)[0], xc, mc[2], w_out_b)
            h2_ctx = rms_norm(xc, norm2[l]) * (1.0 + mc[4]) + mc[3]
            y = hier_moe_pallas(jnp.concatenate([h2_ctx, h2_lat], axis=0), *moe_w)
            xc = xc + mc[5] * y[:n_ctx]
            xl = xl + m[5] * y[n_ctx:]
        else:
            xl = xl + m[5] * hier_moe_pallas(h2_lat, *moe_w)
    return xl[None]
```

```python
import functools
import math

import jax
import jax.numpy as jnp
from jax import lax
from jax.experimental import pallas as pl
from jax.experimental.pallas import tpu as pltpu

D_MODEL = 2048
DEPTH = 2
GRID_W = 64
GROUP_WIDTH = 512
GDN_DK = 64
GDN_DV = 64
GDN_HEADS = GROUP_WIDTH // GDN_DV
SCAN_CHUNK = 64
SC_WIDTH = GROUP_WIDTH
GLA_DK = 64
GLA_DV = 128
GLA_HEADS = GROUP_WIDTH // GLA_DV
GLA_GATE_RANK = 16
GLA_TAU = 16.0
DIFF_DK = 64
DIFF_DV = 128
DIFF_HEADS = GROUP_WIDTH // DIFF_DV
ROPE_BASE = 10000.0
ROPE_PAIRS = DIFF_DK // 4
N_GROUPS = 4
EXPERTS_PER_GROUP = 8
N_EXPERTS = N_GROUPS * EXPERTS_PER_GROUP
TOP_K = 2
D_EXPERT = D_MODEL // 4
EXPERT_BLOCK = 256
NORM_EPS = 1e-6
F32 = jnp.float32
BF16 = jnp.bfloat16
HI = lax.Precision.HIGHEST

PROJ_LAYOUT = (
    ('gdn_qkv', GDN_HEADS * (2 * GDN_DK + GDN_DV)),
    ('gdn_gate', GDN_HEADS * GDN_DV),
    ('gdn_beta', 2 * GDN_HEADS),
    ('gdn_alpha', 2 * GDN_HEADS),
    ('sc_b', SC_WIDTH),
    ('sc_c', SC_WIDTH),
    ('sc_x', SC_WIDTH),
    ('gla_q', GLA_HEADS * GLA_DK),
    ('gla_k', GLA_HEADS * GLA_DK),
    ('gla_v', GLA_HEADS * GLA_DV),
    ('gla_r', GLA_HEADS * GLA_DV),
    ('gla_lowrank', 2 * GLA_GATE_RANK),
    ('diff_q', DIFF_HEADS * 2 * DIFF_DK),
    ('diff_k', DIFF_HEADS * 2 * DIFF_DK),
    ('diff_v', DIFF_HEADS * DIFF_DV),
)
PROJ_OFFSETS = {}
_off = 0
for _name, _size in PROJ_LAYOUT:
    PROJ_OFFSETS[_name] = (_off, _size)
    _off += _size

MAIN_GROUPS = ('gdn_qkv', 'gdn_gate', 'sc_b', 'sc_c', 'sc_x', 'gla_q', 'gla_k', 'gla_v', 'gla_r',
               'diff_q', 'diff_k', 'diff_v')
SMALL_GROUPS = ('gdn_beta', 'gdn_alpha', 'gla_lowrank')
MAIN_OFFSETS = {}
_off = 0
for _name in MAIN_GROUPS:
    MAIN_OFFSETS[_name] = (_off, PROJ_OFFSETS[_name][1])
    _off += PROJ_OFFSETS[_name][1]
MAIN_WIDTH = _off
SMALL_OFFSETS = {}
_off = 0
for _name in SMALL_GROUPS:
    SMALL_OFFSETS[_name] = (_off, PROJ_OFFSETS[_name][1])
    _off += PROJ_OFFSETS[_name][1]
LANE = 128
SMALL_WIDTH = LANE

VMEM_LIMIT = 48 * 1024 * 1024
ROW_TILE = 1024


INPROJ_TM = 1024
INPROJ_TN = 512


def _inproj_body(x_ref, gain_ref, shift_ref, scale_ref, w_ref, ws_ref, o_ref, os_ref, h_ref):
    @pl.when(pl.program_id(1) == 0)
    def _():
        x = x_ref[...]
        y = x * lax.rsqrt(jnp.mean(x * x, axis=-1, keepdims=True) + NORM_EPS) * gain_ref[...]
        hb = (y * (1.0 + scale_ref[...]) + shift_ref[...]).astype(BF16)
        h_ref[...] = hb
        os_ref[...] = jnp.dot(hb, ws_ref[...], preferred_element_type=F32)

    o_ref[...] = jnp.dot(h_ref[...], w_ref[...], preferred_element_type=F32)


def inproj(x, gain, shift, scale, w_main, w_small, n_lat):
    rows = x.shape[0]
    tm = INPROJ_TM
    lat_tiles = n_lat // tm
    mod = pl.BlockSpec((None, 1, D_MODEL), lambda i, j: (i // lat_tiles, 0, 0))
    return pl.pallas_call(
        _inproj_body,
        grid=(rows // tm, MAIN_WIDTH // INPROJ_TN),
        in_specs=[
            pl.BlockSpec((tm, D_MODEL), lambda i, j: (i, 0)),
            pl.BlockSpec((1, D_MODEL), lambda i, j: (0, 0)),
            mod, mod,
            pl.BlockSpec((D_MODEL, INPROJ_TN), lambda i, j: (0, j)),
            pl.BlockSpec((D_MODEL, SMALL_WIDTH), lambda i, j: (0, 0)),
        ],
        out_specs=[
            pl.BlockSpec((tm, INPROJ_TN), lambda i, j: (i, j)),
            pl.BlockSpec((tm, SMALL_WIDTH), lambda i, j: (i, 0)),
        ],
        out_shape=[
            jax.ShapeDtypeStruct((rows, MAIN_WIDTH), F32),
            jax.ShapeDtypeStruct((rows, SMALL_WIDTH), F32),
        ],
        scratch_shapes=[pltpu.VMEM((tm, D_MODEL), BF16)],
        compiler_params=pltpu.CompilerParams(
            dimension_semantics=("parallel", "arbitrary"), vmem_limit_bytes=VMEM_LIMIT),
        name="inproj",
    )(x, gain, shift, scale, w_main, w_small)


OUTPROJ_TM = 512


def _outproj_body(a_ref, x_ref, gate_ref, w_ref, o_ref):
    y = jnp.dot(a_ref[...].astype(BF16), w_ref[...], preferred_element_type=F32)
    o_ref[...] = x_ref[...] + gate_ref[...] * y


def outproj(a, x, gate, w, n_lat):
    rows = x.shape[0]
    tm = OUTPROJ_TM
    lat_tiles = n_lat // tm
    return pl.pallas_call(
        _outproj_body,
        grid=(rows // tm,),
        in_specs=[
            pl.BlockSpec((tm, D_MODEL), lambda i: (i, 0)),
            pl.BlockSpec((tm, D_MODEL), lambda i: (i, 0)),
            pl.BlockSpec((None, 1, D_MODEL), lambda i: (i // lat_tiles, 0, 0)),
            pl.BlockSpec((D_MODEL, D_MODEL), lambda i: (0, 0)),
        ],
        out_specs=pl.BlockSpec((tm, D_MODEL), lambda i: (i, 0)),
        out_shape=jax.ShapeDtypeStruct((rows, D_MODEL), F32),
        compiler_params=pltpu.CompilerParams(
            dimension_semantics=("parallel",), vmem_limit_bytes=VMEM_LIMIT),
        name="outproj",
    )(a, x, gate, w)


HEAD_LANES = 2 * DIFF_DK
ATTN_TK = 512
ATTN_TQ = 512
ATTN_UNROLL = 1
NEG_BIG = -1e30


def _group_mean_sq(x, width):
    n = x.shape[-1]
    gi = lax.broadcasted_iota(jnp.int32, (n, n), 0) // width
    gj = lax.broadcasted_iota(jnp.int32, (n, n), 1) // width
    ones = jnp.where(gi == gj, 1.0, 0.0).astype(F32)
    return jnp.dot(x * x, ones, preferred_element_type=F32, precision=HI) * (1.0 / width)


def _rope_swap(x):
    n = x.shape[-1]
    lane = lax.broadcasted_iota(jnp.int32, x.shape, x.ndim - 1)
    return jnp.where((lane % (2 * ROPE_PAIRS)) < ROPE_PAIRS,
                     pltpu.roll(x, n - ROPE_PAIRS, 1), pltpu.roll(x, ROPE_PAIRS, 1))


def _attn_prep_body(q_ref, k_ref, v_ref, cos_ref, sin_ref, qg_ref, kg_ref, qo_ref, ko_ref, vt_ref):
    cos = jnp.concatenate([cos_ref[...]] * DIFF_HEADS, axis=-1)
    sin = jnp.concatenate([sin_ref[...]] * DIFF_HEADS, axis=-1)

    def norm_rope(x, gain):
        y = x * lax.rsqrt(_group_mean_sq(x, DIFF_DK) + NORM_EPS) * gain
        return y * cos + _rope_swap(y) * sin

    q = norm_rope(q_ref[...], qg_ref[...]) * (DIFF_DK ** -0.5)
    lane = lax.broadcasted_iota(jnp.int32, q.shape, 1)
    first = (lane % HEAD_LANES) < DIFF_DK
    qo_ref[0] = jnp.where(first, q, 0.0).astype(BF16)
    qo_ref[1] = jnp.where(first, 0.0, q).astype(BF16)
    ko_ref[...] = norm_rope(k_ref[...], kg_ref[...]).astype(BF16)
    v = v_ref[...]
    for h in range(DIFF_HEADS):
        vt_ref[h, 0] = v[:, h * DIFF_DV:(h + 1) * DIFF_DV].T.astype(BF16)


def attn_prep(p_main, cos, sin, q_gain, k_gain):
    rows = p_main.shape[0]
    tm = ATTN_TK
    nb = rows // tm
    col = lambda name: MAIN_OFFSETS[name][0] // GROUP_WIDTH
    seg = lambda name: pl.BlockSpec((tm, GROUP_WIDTH), lambda i, c=col(name): (i, c))
    tab = pl.BlockSpec((tm, HEAD_LANES), lambda i: (i, 0))
    vec = pl.BlockSpec((1, GROUP_WIDTH), lambda i: (0, 0))
    return pl.pallas_call(
        _attn_prep_body,
        grid=(nb,),
        in_specs=[seg('diff_q'), seg('diff_k'), seg('diff_v'), tab, tab, vec, vec],
        out_specs=[
            pl.BlockSpec((2, tm, GROUP_WIDTH), lambda i: (0, i, 0)),
            pl.BlockSpec((tm, GROUP_WIDTH), lambda i: (i, 0)),
            pl.BlockSpec((DIFF_HEADS, 1, DIFF_DV, tm), lambda i: (0, i, 0, 0)),
        ],
        out_shape=[
            jax.ShapeDtypeStruct((2, rows, GROUP_WIDTH), BF16),
            jax.ShapeDtypeStruct((rows, GROUP_WIDTH), BF16),
            jax.ShapeDtypeStruct((DIFF_HEADS, nb, DIFF_DV, tm), BF16),
        ],
        compiler_params=pltpu.CompilerParams(dimension_semantics=("parallel",), vmem_limit_bytes=VMEM_LIMIT),
        name="attn_prep",
    )(p_main, p_main, p_main, cos, sin, q_gain, k_gain)


def _attn_body(lam_ref, q_ref, k_ref, vt_ref, gn_ref, o_ref, m_ref, l_ref, acc_ref, *,
               n_lat_chunks, n_ctx, tq, post_scale):
    q2 = q_ref[...].reshape(2 * tq, HEAD_LANES)
    m_ref[...] = jnp.full(m_ref.shape, NEG_BIG, F32)
    l_ref[...] = jnp.zeros(l_ref.shape, F32)
    acc_ref[...] = jnp.zeros(acc_ref.shape, F32)

    def step(kc, vt):
        st = lax.dot_general(kc, q2, (((1,), (1,)), ((), ())), preferred_element_type=F32)
        m_prev = m_ref[...]
        m_new = jnp.maximum(m_prev, jnp.max(st, axis=0, keepdims=True))
        alpha = jnp.exp(m_prev - m_new)
        p = jnp.exp(st - m_new)
        l_ref[...] = alpha * l_ref[...] + jnp.sum(p, axis=0, keepdims=True)
        acc_ref[...] = alpha * acc_ref[...] + jnp.dot(vt, p.astype(BF16), preferred_element_type=F32)
        m_ref[...] = m_new

    if n_lat_chunks:
        def body(j, carry):
            step(k_ref[pl.ds(pl.multiple_of(j * ATTN_TK, ATTN_TK), ATTN_TK), :], vt_ref[j])
            return carry
        lax.fori_loop(0, n_lat_chunks, body, 0, unroll=ATTN_UNROLL)
    step(k_ref[n_lat_chunks * ATTN_TK:n_lat_chunks * ATTN_TK + n_ctx, :], vt_ref[n_lat_chunks][:, :n_ctx])

    acc = acc_ref[...]
    l = l_ref[...]
    o = acc[:, :tq] / l[:, :tq] - lam_ref[0, 0] * (acc[:, tq:] / l[:, tq:])
    y = o * lax.rsqrt(jnp.mean(o * o, axis=0, keepdims=True) + NORM_EPS) * (gn_ref[...] * post_scale)
    o_ref[...] = y.T


def diff_attention(lam, q, k, vt, out_gain, post_scale, n_lat, n_ctx, context_queries):
    lat_chunks = n_lat // ATTN_TK
    if context_queries:
        rows, tq, q_off, n_lat_chunks = n_ctx, n_ctx, n_lat // n_ctx, 0
        k_spec = pl.BlockSpec((ATTN_TK, HEAD_LANES), lambda h, i: (lat_chunks, h))
        vt_spec = pl.BlockSpec((None, 1, DIFF_DV, ATTN_TK), lambda h, i: (h, lat_chunks, 0, 0))
    else:
        rows, tq, q_off, n_lat_chunks = n_lat, ATTN_TQ, 0, lat_chunks
        k_spec = pl.BlockSpec(((lat_chunks + 1) * ATTN_TK, HEAD_LANES), lambda h, i: (0, h))
        vt_spec = pl.BlockSpec((None, lat_chunks + 1, DIFF_DV, ATTN_TK), lambda h, i: (h, 0, 0, 0))
    return pl.pallas_call(
        functools.partial(_attn_body, n_lat_chunks=n_lat_chunks, n_ctx=n_ctx, tq=tq, post_scale=post_scale),
        grid=(DIFF_HEADS, rows // tq),
        in_specs=[
            pl.BlockSpec(memory_space=pltpu.SMEM),
            pl.BlockSpec((2, tq, HEAD_LANES), lambda h, i: (0, q_off + i, h)),
            k_spec, vt_spec,
            pl.BlockSpec((DIFF_DV, 1), lambda h, i: (0, 0)),
        ],
        out_specs=pl.BlockSpec((tq, DIFF_DV), lambda h, i: (i, h)),
        out_shape=jax.ShapeDtypeStruct((rows, GROUP_WIDTH), F32),
        scratch_shapes=[
            pltpu.VMEM((1, 2 * tq), F32),
            pltpu.VMEM((1, 2 * tq), F32),
            pltpu.VMEM((DIFF_DV, 2 * tq), F32),
        ],
        compiler_params=pltpu.CompilerParams(
            dimension_semantics=("parallel", "arbitrary"), vmem_limit_bytes=VMEM_LIMIT),
        name="diff_attn_ctx" if context_queries else "diff_attn_lat",
    )(lam, q, k, vt, out_gain)


def rope_tables(n_lat, rows):
    pos = jnp.arange(n_lat, dtype=jnp.int32)
    row = (pos // GRID_W).astype(F32)
    col = (pos % GRID_W).astype(F32)
    inv_freq = ROPE_BASE ** (-jnp.arange(ROPE_PAIRS, dtype=F32) / ROPE_PAIRS)
    ang_r, ang_c = row[:, None] * inv_freq, col[:, None] * inv_freq
    cos = jnp.concatenate([jnp.cos(ang_r), jnp.cos(ang_r), jnp.cos(ang_c), jnp.cos(ang_c)], axis=-1)
    sin = jnp.concatenate([-jnp.sin(ang_r), jnp.sin(ang_r), -jnp.sin(ang_c), jnp.sin(ang_c)], axis=-1)
    cos, sin = jnp.tile(cos, (1, 2)), jnp.tile(sin, (1, 2))
    pad = ((0, rows - n_lat), (0, 0))
    return jnp.pad(cos, pad, constant_values=1.0), jnp.pad(sin, pad)


def diff_lambda_scalar(lam_params, lam_init):
    lq1, lk1, lq2, lk2 = lam_params
    return (jnp.exp(jnp.sum(lq1 * lk1)) - jnp.exp(jnp.sum(lq2 * lk2)) + lam_init).reshape(1, 1)


GDN_QKV_WIDTH = GDN_HEADS * (2 * GDN_DK + GDN_DV)
GDN_PREP_TM = 512
HALO = 8


def _softplus(z):
    return jnp.maximum(z, 0.0) + jnp.log(1.0 + jnp.exp(-jnp.abs(z)))


def _conv3(x, prev_row, next_row, w, row0, n_lat, n_tok):
    tm = x.shape[0]
    r = lax.broadcasted_iota(jnp.int32, (tm, 1), 0)
    rows = row0 + r
    xm1 = jnp.where(r == 0, prev_row, pltpu.roll(x, 1, 0))
    xm1 = jnp.where((rows == 0) | (rows == n_lat), 0.0, xm1)
    xp1 = jnp.where(r == tm - 1, next_row, pltpu.roll(x, tm - 1, 0))
    xp1 = jnp.where((rows == n_lat - 1) | (rows == n_tok - 1), 0.0, xp1)
    return xm1 * w[0:1] + x * w[1:2] + xp1 * w[2:3]


def _gdn_prep_body(x_ref, xp_ref, xn_ref, ps_ref, cw_ref, a_ref, dt_ref, q_ref, k_ref, v_ref, gs_ref, *, n_lat, n_tok):
    tm = x_ref.shape[0]
    y = _conv3(x_ref[...], xp_ref[HALO - 1:HALO, :], xn_ref[0:1, :], cw_ref[...], pl.program_id(0) * tm, n_lat, n_tok)
    y = y * jax.nn.sigmoid(y)
    hk = GDN_HEADS * GDN_DK
    q, k = y[:, :hk], y[:, hk:2 * hk]
    q_ref[...] = q * lax.rsqrt(_group_mean_sq(q, GDN_DK) * GDN_DK + NORM_EPS) * (GDN_DK ** -0.5)
    k_ref[...] = k * lax.rsqrt(_group_mean_sq(k, GDN_DK) * GDN_DK + NORM_EPS)
    v_ref[...] = y[:, 2 * hk:]
    ps = ps_ref[...]
    lane = lax.broadcasted_iota(jnp.int32, ps.shape, 1)
    gs_ref[...] = jnp.where(lane < 2 * GDN_HEADS, jax.nn.sigmoid(ps), a_ref[...] * _softplus(ps + dt_ref[...]))


def gdn_prep(p_main, p_small, conv_w, neg_a, dt, n_lat, n_tok):
    rows = p_main.shape[0]
    tm = GDN_PREP_TM
    nb = rows // tm
    per = tm // HALO
    out = jax.ShapeDtypeStruct((rows, GROUP_WIDTH), F32)
    return pl.pallas_call(
        functools.partial(_gdn_prep_body, n_lat=n_lat, n_tok=n_tok),
        grid=(nb,),
        in_specs=[
            pl.BlockSpec((tm, GDN_QKV_WIDTH), lambda i: (i, 0)),
            pl.BlockSpec((HALO, GDN_QKV_WIDTH), lambda i: (jnp.maximum(i * per - 1, 0), 0)),
            pl.BlockSpec((HALO, GDN_QKV_WIDTH), lambda i: (jnp.minimum((i + 1) * per, nb * per - 1), 0)),
            pl.BlockSpec((tm, SMALL_WIDTH), lambda i: (i, 0)),
            pl.BlockSpec((3, GDN_QKV_WIDTH), lambda i: (0, 0)),
            pl.BlockSpec((1, SMALL_WIDTH), lambda i: (0, 0)),
            pl.BlockSpec((1, SMALL_WIDTH), lambda i: (0, 0)),
        ],
        out_specs=[pl.BlockSpec((tm, GROUP_WIDTH), lambda i: (i, 0))] * 3 + [pl.BlockSpec((tm, SMALL_WIDTH), lambda i: (i, 0))],
        out_shape=[out, out, out, jax.ShapeDtypeStruct((rows, SMALL_WIDTH), F32)],
        compiler_params=pltpu.CompilerParams(dimension_semantics=("parallel",), vmem_limit_bytes=VMEM_LIMIT),
        name="gdn_prep",
    )(p_main, p_main, p_main, p_small, conv_w, neg_a, dt)


PAIR = 2 * SCAN_CHUNK


def _stack_pair(a):
    first = lax.broadcasted_iota(jnp.int32, a.shape, 1) < (PAIR // 2)
    return jnp.concatenate([jnp.where(first, a, 0.0), jnp.where(first, 0.0, a)], axis=0)


def _pair_col(arr, c0, c1):
    return jnp.concatenate([arr[:, c0:c0 + 1], arr[:, c1:c1 + 1]], axis=0)


def _dot_nt(a, b):
    return lax.dot_general(a, b, (((1,), (1,)), ((), ())), preferred_element_type=F32)


def _dot_tn(a, b):
    return lax.dot_general(a, b, (((0,), (0,)), ((), ())), preferred_element_type=F32)


def _chunk_masks(d):
    C = SCAN_CHUNK
    r = lax.broadcasted_iota(jnp.int32, (PAIR, PAIR), 0)
    c = lax.broadcasted_iota(jnp.int32, (PAIR, PAIR), 1)
    same = (r // C) == (c // C)
    ri, ci = r % C, c % C
    incl = same & ((ci <= ri) if d == 0 else (ci >= ri))
    strict = same & ((ci < ri) if d == 0 else (ci > ri))
    r1 = lax.broadcasted_iota(jnp.int32, (C, C), 0)
    c1 = lax.broadcasted_iota(jnp.int32, (C, C), 1)
    tri = jnp.where((c1 <= r1) if d == 0 else (c1 >= r1), 1.0, 0.0).astype(F32)
    return incl, strict, tri, r == c


def _gdn_scan_body(qf, kf, vf, gf, qb, kb, vb, gb, of_ref, ob_ref, s_ref):
    C = SCAN_CHUNK
    dot = functools.partial(jnp.dot, preferred_element_type=F32)

    @pl.when(pl.program_id(0) == 0)
    def _():
        s_ref[...] = jnp.zeros(s_ref.shape, F32)

    top = lax.broadcasted_iota(jnp.int32, (PAIR, 1), 0) < C
    units = []
    for d, (q_ref, k_ref, v_ref, g_ref, o_ref) in enumerate(((qf, kf, vf, gf, of_ref), (qb, kb, vb, gb, ob_ref))):
        incl, strict, tri, eye = _chunk_masks(d)
        gs = g_ref[...]
        g_all = jnp.dot(tri, gs, preferred_element_type=F32, precision=HI)
        g_end = g_all[C - 1:C] if d == 0 else g_all[0:1]
        g_t = g_all.T
        for p in range(GDN_HEADS // 2):
            sl = slice(PAIR * p, PAIR * (p + 1))
            b0 = GDN_HEADS * d + 2 * p
            c0 = 2 * GDN_HEADS + b0
            u = dict(d=d, p=p, sl=sl, o_ref=o_ref, incl=incl, strict=strict, eye=eye)
            u['K'], u['Q'], u['V'] = _stack_pair(k_ref[:, sl]), _stack_pair(q_ref[:, sl]), _stack_pair(v_ref[:, sl])
            u['g_col'] = _pair_col(g_all, c0, c0 + 1)
            u['beta'] = _pair_col(gs, b0, b0 + 1)
            u['g_row'] = jnp.concatenate([g_t[c0:c0 + 1, :], g_t[c0 + 1:c0 + 2, :]], axis=1)
            u['gl_col'] = jnp.where(top, g_end[:, c0:c0 + 1], g_end[:, c0 + 1:c0 + 2])
            units.append(u)
    for u in units:
        u['decay'] = jnp.exp(jnp.where(u['incl'], u['g_col'] - u['g_row'], NEG_BIG))
        u['eg'] = jnp.exp(u['g_col'])
    for u in units:
        u['A'] = jnp.where(u['strict'], -(u['beta'] * _dot_nt(u['K'], u['K']) * u['decay']), 0.0)
        u['P'] = jnp.where(u['eye'], 1.0, 0.0) + u['A']
        u['negL'] = u['A']
    for u in units:
        u['attn'] = _dot_nt(u['Q'], u['K']) * u['decay']
    for _ in range(5):
        for u in units:
            u['A'] = dot(u['A'], u['A'])
        for u in units:
            u['P'] = u['P'] + dot(u['P'], u['A'])
    for u in units:
        u['R'] = (jnp.where(u['eye'], 1.0, 0.0) - u['P']) + jnp.dot(u['negL'], u['P'], preferred_element_type=F32, precision=HI)
    for u in units:
        u['P'] = u['P'] + dot(u['P'], u['R'])
    for u in units:
        u['uw'] = dot(u['P'], jnp.concatenate([u['V'] * u['beta'], u['K'] * (u['beta'] * u['eg'])], axis=1))
    for u in units:
        u['S'] = s_ref[u['d'], u['p']]
        u['ws'] = dot(jnp.concatenate([u['uw'][:, PAIR:], u['Q'] * u['eg']], axis=0), u['S'])
    for u in units:
        u['v_new'] = u['uw'][:, :PAIR] - u['ws'][:PAIR]
        o_st = u['ws'][PAIR:] + dot(u['attn'], u['v_new'])
        u['o_ref'][:, u['sl']] = o_st[:C] + o_st[C:]
    for u in units:
        s_ref[u['d'], u['p']] = (u['S'] * jnp.exp(u['gl_col'])
                                 + _dot_tn(u['K'] * jnp.exp(u['gl_col'] - u['g_col']), u['v_new']))


def _scan_order(n_lat, n_ctx):
    nl, nc = n_lat // SCAN_CHUNK, n_ctx // SCAN_CHUNK
    fwd = lambda t: (jnp.where(t < nc, nl + t, t - nc), 0)
    bwd = lambda t: (nl + nc - 1 - t, 0)
    return nl + nc, fwd, bwd


def gdn_scan(q, k, v, gs, n_lat, n_ctx):
    steps, fwd, bwd = _scan_order(n_lat, n_ctx)
    wide = lambda f: pl.BlockSpec((SCAN_CHUNK, GROUP_WIDTH), f)
    small = lambda f: pl.BlockSpec((SCAN_CHUNK, SMALL_WIDTH), f)
    out = jax.ShapeDtypeStruct((n_lat + n_ctx, GROUP_WIDTH), F32)
    return pl.pallas_call(
        _gdn_scan_body,
        grid=(steps,),
        in_specs=[wide(fwd), wide(fwd), wide(fwd), small(fwd), wide(bwd), wide(bwd), wide(bwd), small(bwd)],
        out_specs=[wide(fwd), wide(bwd)],
        out_shape=[out, out],
        scratch_shapes=[pltpu.VMEM((2, GDN_HEADS // 2, PAIR, PAIR), F32)],
        compiler_params=pltpu.CompilerParams(dimension_semantics=("arbitrary",), vmem_limit_bytes=VMEM_LIMIT),
        name="gdn_scan",
    )(q, k, v, gs, q, k, v, gs)


def gdn_gate_vectors(a_log, dt_bias):
    n = 2 * GDN_HEADS
    pad = lambda t: jnp.pad(t.reshape(1, n), ((0, 0), (n, SMALL_WIDTH - 2 * n)))
    return pad(-jnp.exp(a_log)), pad(dt_bias)


GLA_QK_WIDTH = GLA_HEADS * GLA_DK


def _gla_scan_body(qf, kf, vf, pf, qb, kb, vb, pb, wup_ref, bias_ref, of_ref, ob_ref, s_ref):
    C = SCAN_CHUNK
    dot = functools.partial(jnp.dot, preferred_element_type=F32)

    @pl.when(pl.program_id(0) == 0)
    def _():
        s_ref[...] = jnp.zeros(s_ref.shape, F32)

    units = []
    for d, (q_ref, k_ref, v_ref, p_ref, o_ref) in enumerate(((qf, kf, vf, pf, of_ref), (qb, kb, vb, pb, ob_ref))):
        incl, _, tri, _ = _chunk_masks(d)
        z = dot(p_ref[...], wup_ref[...]) + bias_ref[...]
        z = z[:, GLA_QK_WIDTH * d:GLA_QK_WIDTH * (d + 1)]
        log_a = (jnp.minimum(z, 0.0) - jnp.log(1.0 + jnp.exp(-jnp.abs(z)))) * (1.0 / GLA_TAU)
        b = jnp.dot(tri, log_a, preferred_element_type=F32, precision=HI)
        b_end = b[C - 1:C] if d == 0 else b[0:1]
        q = q_ref[...] * (GLA_DK ** -0.5)
        k_out = k_ref[...] * jnp.exp(b_end - b)
        q_in = q * jnp.exp(b - b_end)
        q_st = q * jnp.exp(b)
        a_col = jnp.broadcast_to(jnp.exp(b_end), (C, GLA_QK_WIDTH)).T[:, 0:1]
        v = v_ref[...]
        for p in range(GLA_HEADS // 2):
            sl = slice(PAIR * p, PAIR * (p + 1))
            v0, v1 = v[:, 2 * p * GLA_DV:(2 * p + 1) * GLA_DV], v[:, (2 * p + 1) * GLA_DV:(2 * p + 2) * GLA_DV]
            units.append(dict(d=d, p=p, o_ref=o_ref, incl=incl, V=jnp.concatenate([v0, v1], axis=0),
                              K=_stack_pair(k_out[:, sl]), Qin=_stack_pair(q_in[:, sl]), Qst=_stack_pair(q_st[:, sl]),
                              a=a_col[sl]))
    for u in units:
        u['attn'] = jnp.where(u['incl'], _dot_nt(u['Qin'], u['K']), 0.0)
        u['S'] = s_ref[u['d'], u['p']]
    for u in units:
        u['o'] = dot(u['Qst'], u['S'])
        u['dS'] = _dot_tn(u['K'], u['V'])
    for u in units:
        o_st = u['o'] + dot(u['attn'], u['V'])
        p = u['p']
        u['o_ref'][:, 2 * p * GLA_DV:(2 * p + 1) * GLA_DV] = o_st[:C]
        u['o_ref'][:, (2 * p + 1) * GLA_DV:(2 * p + 2) * GLA_DV] = o_st[C:]
        s_ref[u['d'], p] = u['a'] * u['S'] + u['dS']


def gla_scan(p_main, p_small, w_up, bias, n_lat, n_ctx):
    steps, fwd, bwd = _scan_order(n_lat, n_ctx)
    qc, kc, vc = (MAIN_OFFSETS[n][0] for n in ('gla_q', 'gla_k', 'gla_v'))
    at = lambda f, blk: (lambda t: (f(t)[0], blk))
    qk = lambda f, off: pl.BlockSpec((SCAN_CHUNK, GLA_QK_WIDTH), at(f, off // GLA_QK_WIDTH))
    vv = lambda f: pl.BlockSpec((SCAN_CHUNK, GROUP_WIDTH), at(f, vc // GROUP_WIDTH))
    small = lambda f: pl.BlockSpec((SCAN_CHUNK, SMALL_WIDTH), f)
    const = lambda shape: pl.BlockSpec(shape, lambda t: (0, 0))
    out = jax.ShapeDtypeStruct((n_lat + n_ctx, GROUP_WIDTH), F32)
    return pl.pallas_call(
        _gla_scan_body,
        grid=(steps,),
        in_specs=[qk(fwd, qc), qk(fwd, kc), vv(fwd), small(fwd), qk(bwd, qc), qk(bwd, kc), vv(bwd), small(bwd),
                  const((SMALL_WIDTH, 2 * GLA_QK_WIDTH)), const((1, 2 * GLA_QK_WIDTH))],
        out_specs=[pl.BlockSpec((SCAN_CHUNK, GROUP_WIDTH), fwd), pl.BlockSpec((SCAN_CHUNK, GROUP_WIDTH), bwd)],
        out_shape=[out, out],
        scratch_shapes=[pltpu.VMEM((2, GLA_HEADS // 2, PAIR, GLA_DV), F32)],
        compiler_params=pltpu.CompilerParams(dimension_semantics=("arbitrary",), vmem_limit_bytes=VMEM_LIMIT),
        name="gla_scan",
    )(p_main, p_main, p_main, p_small, p_main, p_main, p_main, p_small, w_up, bias)


def gla_gate_weights(gate_up, gate_bias):
    off = SMALL_OFFSETS['gla_lowrank'][0]
    w = jnp.zeros((SMALL_WIDTH, 2 * GLA_QK_WIDTH), F32)
    for d in range(2):
        w = w.at[off + d * GLA_GATE_RANK:off + (d + 1) * GLA_GATE_RANK, d * GLA_QK_WIDTH:(d + 1) * GLA_QK_WIDTH].set(gate_up[d])
    return w, gate_bias.reshape(1, 2 * GLA_QK_WIDTH)


def _ffn_body(be_ref, nu_ref, x_ref, w1_ref, w3_ref, w2_ref, o_ref, w1b_ref, w3b_ref, w2b_ref):
    b = pl.program_id(0)
    used = b < nu_ref[0]
    new_expert = jnp.logical_or(b == 0, be_ref[b] != be_ref[jnp.maximum(b - 1, 0)])

    @pl.when(jnp.logical_and(used, new_expert))
    def _():
        w1b_ref[...] = w1_ref[...].astype(BF16)
        w3b_ref[...] = w3_ref[...].astype(BF16)
        w2b_ref[...] = w2_ref[...].astype(BF16)

    @pl.when(used)
    def _():
        x = x_ref[...]
        a = jnp.dot(x, w1b_ref[...], preferred_element_type=F32)
        g = jnp.dot(x, w3b_ref[...], preferred_element_type=F32)
        mid = (a * jax.nn.sigmoid(a) * g).astype(BF16)
        o_ref[...] = jnp.dot(mid, w2b_ref[...], preferred_element_type=F32)

    @pl.when(jnp.logical_not(used))
    def _():
        o_ref[...] = jnp.zeros(o_ref.shape, F32)


def expert_ffn(block_expert, n_used, xs, w1, w3, w2):
    L = xs.shape[0]
    n_blocks = L // EXPERT_BLOCK
    wspec = lambda shape: pl.BlockSpec((None,) + shape, lambda b, be, nu: (be[b], 0, 0))
    return pl.pallas_call(
        _ffn_body,
        grid_spec=pltpu.PrefetchScalarGridSpec(
            num_scalar_prefetch=2,
            grid=(n_blocks,),
            in_specs=[
                pl.BlockSpec((EXPERT_BLOCK, D_MODEL), lambda b, be, nu: (b, 0)),
                wspec((D_MODEL, D_EXPERT)), wspec((D_MODEL, D_EXPERT)), wspec((D_EXPERT, D_MODEL)),
            ],
            out_specs=pl.BlockSpec((EXPERT_BLOCK, D_MODEL), lambda b, be, nu: (b, 0)),
            scratch_shapes=[
                pltpu.VMEM((D_MODEL, D_EXPERT), BF16),
                pltpu.VMEM((D_MODEL, D_EXPERT), BF16),
                pltpu.VMEM((D_EXPERT, D_MODEL), BF16),
            ],
        ),
        out_shape=jax.ShapeDtypeStruct((L, D_MODEL), F32),
        compiler_params=pltpu.CompilerParams(dimension_semantics=("arbitrary",), vmem_limit_bytes=VMEM_LIMIT),
        name="expert_ffn",
    )(block_expert, n_used, xs, w1, w3, w2)


def hier_moe_pallas(hf, wg, bg, we, be, w1, w3, w2):
    N, D = hf.shape
    g_prob = jax.nn.softmax(jnp.dot(hf, wg, precision=HI) + bg, axis=-1)
    g_top_p, g_idx = lax.top_k(g_prob, 1)
    e_logits = (jnp.dot(hf, we, precision=HI) + be).reshape(N, N_GROUPS, EXPERTS_PER_GROUP)
    e_in_group = jnp.take_along_axis(e_logits, g_idx[:, :, None], axis=1)[:, 0]
    e_top_p, e_top_i = lax.top_k(jax.nn.softmax(e_in_group, axis=-1), TOP_K)
    e_top_p = e_top_p / jnp.sum(e_top_p, axis=-1, keepdims=True)
    weights = g_top_p * e_top_p
    expert_idx = g_idx * EXPERTS_PER_GROUP + e_top_i

    E, M, K = N_EXPERTS, EXPERT_BLOCK, TOP_K
    A = N * K
    flat_e = expert_idx.reshape(-1).astype(jnp.int32)
    order = jnp.argsort(flat_e)
    e_s = flat_e[order]
    t_s = (order // K).astype(jnp.int32)
    counts = jnp.bincount(flat_e, length=E).astype(jnp.int32)
    padded = (counts + M - 1) // M * M
    pad_end = jnp.cumsum(padded)
    pad_start = pad_end - padded
    start = jnp.cumsum(counts) - counts
    dest = pad_start[e_s] + (jnp.arange(A, dtype=jnp.int32) - start[e_s])
    n_blocks = -(-A // M) + E
    L = n_blocks * M
    slot_tok = jnp.full((L,), N, jnp.int32).at[dest].set(t_s)
    block_expert = jnp.minimum(
        jnp.searchsorted(pad_end, jnp.arange(n_blocks, dtype=jnp.int32) * M, side='right'), E - 1).astype(jnp.int32)
    n_used = (pad_end[-1] // M).astype(jnp.int32).reshape(1)
    h_pad = jnp.concatenate([hf.astype(BF16), jnp.zeros((1, D), BF16)], axis=0)
    xs = h_pad[slot_tok]
    y = expert_ffn(block_expert, n_used, xs, w1, w3, w2)
    pos = jnp.zeros((A,), jnp.int32).at[order].set(dest).reshape(N, K)
    return y[pos[:, 0]] * weights[:, 0:1] + y[pos[:, 1]] * weights[:, 1:2]


def rms_norm(x, gain):
    y = x * lax.rsqrt(jnp.mean(x * x, axis=-1, keepdims=True) + NORM_EPS)
    return y * gain


def _head_norm_gate(o, gain, gate, head_dim):
    rows = o.shape[0]
    y = rms_norm(o.reshape(rows, -1, head_dim), gain).reshape(rows, -1)
    return y * jax.nn.silu(gate)


def _short_conv(pm, w, n_lat, n_tok):
    seg = lambda name: pm[:, MAIN_OFFSETS[name][0]:MAIN_OFFSETS[name][0] + MAIN_OFFSETS[name][1]]
    u = seg('sc_c') * seg('sc_x')
    rows = jnp.arange(pm.shape[0], dtype=jnp.int32)[:, None]
    um1 = jnp.where((rows == 0) | (rows == n_lat), 0.0, jnp.roll(u, 1, axis=0))
    up1 = jnp.where((rows == n_lat - 1) | (rows == n_tok - 1), 0.0, jnp.roll(u, -1, axis=0))
    return seg('sc_b') * (um1 * w[0] + u * w[1] + up1 * w[2])


def _prep_w_in(w):
    cols = lambda names: jnp.concatenate(
        [w[:, PROJ_OFFSETS[n][0]:PROJ_OFFSETS[n][0] + PROJ_OFFSETS[n][1]] for n in names], axis=1)
    small = cols(SMALL_GROUPS)
    small = jnp.pad(small, ((0, 0), (0, SMALL_WIDTH - small.shape[1])))
    return cols(MAIN_GROUPS).astype(BF16), small.astype(BF16)


def kernel(x, c, ctx, c_ctx, w_mod, b_mod, norm1, norm2, w_in, w_out, gdn_conv, gdn_a_log, gdn_dt_bias, gdn_out_norm, sc_conv, gla_gate_up, gla_gate_bias, gla_out_norm, diff_q_norm, diff_k_norm, diff_lambda, diff_out_norm, router_group, router_group_bias, router_expert, router_expert_bias, expert_w1, expert_w3, expert_w2):
    n_lat, n_ctx = x.shape[1], ctx.shape[1]
    n_tok = n_lat + n_ctx
    rows = -(-n_tok // ROW_TILE) * ROW_TILE
    pad_rows = lambda t: jnp.pad(t, ((0, rows - t.shape[0]), (0, 0)))
    cos, sin = rope_tables(n_lat, rows)
    is_lat = jnp.arange(rows, dtype=jnp.int32)[:, None] < n_lat

    xa = pad_rows(jnp.concatenate([x[0], ctx[0]], axis=0))
    cond, cond_ctx = jax.nn.silu(c), jax.nn.silu(c_ctx)
    for l in range(DEPTH):
        with_ctx_out = l < DEPTH - 1
        lam_init = 0.8 - 0.6 * math.exp(-0.3 * l)
        m = jnp.split(cond @ w_mod[l] + b_mod[l], 6, axis=-1)
        mc = jnp.split((cond_ctx @ w_mod[l] + b_mod[l])[None, :], 6, axis=-1)
        mod = [jnp.stack([m[i], mc[i]], axis=0) for i in range(6)]

        w_main, w_small = _prep_w_in(w_in[l])
        pm, ps = inproj(xa, norm1[l][None, :], mod[0], mod[1], w_main, w_small, n_lat)
        seg = lambda name: pm[:, MAIN_OFFSETS[name][0]:MAIN_OFFSETS[name][0] + MAIN_OFFSETS[name][1]]

        neg_a, dt = gdn_gate_vectors(gdn_a_log[l], gdn_dt_bias[l])
        gq, gk, gv, gs = gdn_prep(pm, ps, gdn_conv[l], neg_a, dt, n_lat, n_tok)
        gdn_f, gdn_b = gdn_scan(gq, gk, gv, gs, n_lat, n_ctx)
        a_mix = _head_norm_gate(pad_rows(gdn_f + gdn_b), gdn_out_norm[l], seg('gdn_gate'), GDN_DV)

        b_mix = _short_conv(pm, sc_conv[l], n_lat, n_tok)

        w_up, up_bias = gla_gate_weights(gla_gate_up[l], gla_gate_bias[l])
        gla_f, gla_b = gla_scan(pm, ps, w_up, up_bias, n_lat, n_ctx)
        g_mix = _head_norm_gate(pad_rows(gla_f + gla_b), gla_out_norm[l], seg('gla_r'), GLA_DV)

        lam = diff_lambda_scalar(diff_lambda[l], lam_init)
        q_gain = jnp.tile(diff_q_norm[l], 2 * DIFF_HEADS)[None, :]
        k_gain = jnp.tile(diff_k_norm[l], 2 * DIFF_HEADS)[None, :]
        d_gain = diff_out_norm[l][:, None]
        dq, dk, dvt = attn_prep(pm, cos, sin, q_gain, k_gain)
        d_mix = diff_attention(lam, dq, dk, dvt, d_gain, 1.0 - lam_init, n_lat, n_ctx, False)
        if with_ctx_out:
            d_ctx = diff_attention(lam, dq, dk, dvt, d_gain, 1.0 - lam_init, n_lat, n_ctx, True)
            d_mix = jnp.concatenate([d_mix, d_ctx], axis=0)
        d_mix = pad_rows(d_mix)

        mix = jnp.concatenate([a_mix, b_mix, g_mix, d_mix], axis=-1)
        xa = outproj(mix, xa, mod[2], w_out[l].astype(BF16), n_lat)

        h2 = rms_norm(xa, norm2[l]) * (1.0 + jnp.where(is_lat, m[4], mc[4])) + jnp.where(is_lat, m[3], mc[3])
        n_moe = n_tok if with_ctx_out else n_lat
        y = hier_moe_pallas(h2[:n_moe], router_group[l], router_group_bias[l], router_expert[l],
                            router_expert_bias[l], expert_w1[l], expert_w3[l], expert_w2[l])
        xa = xa + jnp.where(is_lat, m[5], mc[5]) * pad_rows(y)
    return xa[:n_lat][None]
```

```python
import functools
import math

import jax
import jax.numpy as jnp
from jax import lax
from jax.experimental import pallas as pl
from jax.experimental.pallas import tpu as pltpu

D_MODEL = 2048
DEPTH = 2
GRID_W = 64
GROUP_WIDTH = 512
GDN_DK = 64
GDN_DV = 64
GDN_HEADS = GROUP_WIDTH // GDN_DV
SCAN_CHUNK = 64
SC_WIDTH = GROUP_WIDTH
GLA_DK = 64
GLA_DV = 128
GLA_HEADS = GROUP_WIDTH // GLA_DV
GLA_GATE_RANK = 16
GLA_TAU = 16.0
DIFF_DK = 64
DIFF_DV = 128
DIFF_HEADS = GROUP_WIDTH // DIFF_DV
ROPE_BASE = 10000.0
ROPE_PAIRS = DIFF_DK // 4
N_GROUPS = 4
EXPERTS_PER_GROUP = 8
N_EXPERTS = N_GROUPS * EXPERTS_PER_GROUP
TOP_K = 2
D_EXPERT = D_MODEL // 4
EXPERT_BLOCK = 256
NORM_EPS = 1e-6
F32 = jnp.float32
BF16 = jnp.bfloat16
HI = lax.Precision.HIGHEST

PROJ_LAYOUT = (
    ('gdn_qkv', GDN_HEADS * (2 * GDN_DK + GDN_DV)),
    ('gdn_gate', GDN_HEADS * GDN_DV),
    ('gdn_beta', 2 * GDN_HEADS),
    ('gdn_alpha', 2 * GDN_HEADS),
    ('sc_b', SC_WIDTH),
    ('sc_c', SC_WIDTH),
    ('sc_x', SC_WIDTH),
    ('gla_q', GLA_HEADS * GLA_DK),
    ('gla_k', GLA_HEADS * GLA_DK),
    ('gla_v', GLA_HEADS * GLA_DV),
    ('gla_r', GLA_HEADS * GLA_DV),
    ('gla_lowrank', 2 * GLA_GATE_RANK),
    ('diff_q', DIFF_HEADS * 2 * DIFF_DK),
    ('diff_k', DIFF_HEADS * 2 * DIFF_DK),
    ('diff_v', DIFF_HEADS * DIFF_DV),
)
PROJ_OFFSETS = {}
_off = 0
for _name, _size in PROJ_LAYOUT:
    PROJ_OFFSETS[_name] = (_off, _size)
    _off += _size

MAIN_GROUPS = ('gdn_qkv', 'gdn_gate', 'sc_b', 'sc_c', 'sc_x', 'gla_q', 'gla_k', 'gla_v', 'gla_r',
               'diff_q', 'diff_k', 'diff_v')
SMALL_GROUPS = ('gdn_beta', 'gdn_alpha', 'gla_lowrank')
MAIN_OFFSETS = {}
_off = 0
for _name in MAIN_GROUPS:
    MAIN_OFFSETS[_name] = (_off, PROJ_OFFSETS[_name][1])
    _off += PROJ_OFFSETS[_name][1]
MAIN_WIDTH = _off
SMALL_OFFSETS = {}
_off = 0
for _name in SMALL_GROUPS:
    SMALL_OFFSETS[_name] = (_off, PROJ_OFFSETS[_name][1])
    _off += PROJ_OFFSETS[_name][1]
LANE = 128
SMALL_WIDTH = LANE

VMEM_LIMIT = 48 * 1024 * 1024
ROW_TILE = 1024


INPROJ_TM = 1024
INPROJ_TN = 512


def _inproj_body(x_ref, gain_ref, shift_ref, scale_ref, w_ref, ws_ref, o_ref, os_ref, h_ref):
    @pl.when(pl.program_id(1) == 0)
    def _():
        x = x_ref[...]
        y = x * lax.rsqrt(jnp.mean(x * x, axis=-1, keepdims=True) + NORM_EPS) * gain_ref[...]
        hb = (y * (1.0 + scale_ref[...]) + shift_ref[...]).astype(BF16)
        h_ref[...] = hb
        os_ref[...] = jnp.dot(hb, ws_ref[...], preferred_element_type=F32)

    o_ref[...] = jnp.dot(h_ref[...], w_ref[...], preferred_element_type=F32)


def inproj(x, gain, shift, scale, w_main, w_small, n_lat):
    rows = x.shape[0]
    tm = INPROJ_TM
    lat_tiles = n_lat // tm
    mod = pl.BlockSpec((None, 1, D_MODEL), lambda i, j: (i // lat_tiles, 0, 0))
    return pl.pallas_call(
        _inproj_body,
        grid=(rows // tm, MAIN_WIDTH // INPROJ_TN),
        in_specs=[
            pl.BlockSpec((tm, D_MODEL), lambda i, j: (i, 0)),
            pl.BlockSpec((1, D_MODEL), lambda i, j: (0, 0)),
            mod, mod,
            pl.BlockSpec((D_MODEL, INPROJ_TN), lambda i, j: (0, j)),
            pl.BlockSpec((D_MODEL, SMALL_WIDTH), lambda i, j: (0, 0)),
        ],
        out_specs=[
            pl.BlockSpec((tm, INPROJ_TN), lambda i, j: (i, j)),
            pl.BlockSpec((tm, SMALL_WIDTH), lambda i, j: (i, 0)),
        ],
        out_shape=[
            jax.ShapeDtypeStruct((rows, MAIN_WIDTH), F32),
            jax.ShapeDtypeStruct((rows, SMALL_WIDTH), F32),
        ],
        scratch_shapes=[pltpu.VMEM((tm, D_MODEL), BF16)],
        compiler_params=pltpu.CompilerParams(
            dimension_semantics=("parallel", "arbitrary"), vmem_limit_bytes=VMEM_LIMIT),
        name="inproj",
    )(x, gain, shift, scale, w_main, w_small)


HEAD_LANES = 2 * DIFF_DK
ATTN_TK = 256
ATTN_TQ = 1024
BF16_SUBLANES = 16
DV_EXT = DIFF_DV + BF16_SUBLANES
LOG2E = math.log2(math.e)
NEG_BIG = -1e30


def _group_mean_sq(x, width):
    n = x.shape[-1]
    gi = lax.broadcasted_iota(jnp.int32, (n, n), 0) // width
    gj = lax.broadcasted_iota(jnp.int32, (n, n), 1) // width
    ones = jnp.where(gi == gj, 1.0, 0.0).astype(F32)
    return jnp.dot(x * x, ones, preferred_element_type=F32, precision=HI) * (1.0 / width)


def _rope_swap(x):
    n = x.shape[-1]
    lane = lax.broadcasted_iota(jnp.int32, x.shape, x.ndim - 1)
    return jnp.where((lane % (2 * ROPE_PAIRS)) < ROPE_PAIRS,
                     pltpu.roll(x, n - ROPE_PAIRS, 1), pltpu.roll(x, ROPE_PAIRS, 1))


def _attn_prep_body(q_ref, k_ref, v_ref, cos_ref, sin_ref, qg_ref, kg_ref, qo_ref, ko_ref, vt_ref):
    cos = jnp.concatenate([cos_ref[...]] * DIFF_HEADS, axis=-1)
    sin = jnp.concatenate([sin_ref[...]] * DIFF_HEADS, axis=-1)

    def norm_rope(x, gain):
        y = x * lax.rsqrt(_group_mean_sq(x, DIFF_DK) + NORM_EPS) * gain
        return y * cos + _rope_swap(y) * sin

    q = norm_rope(q_ref[...], qg_ref[...]) * (DIFF_DK ** -0.5 * LOG2E)
    lane = lax.broadcasted_iota(jnp.int32, q.shape, 1)
    first = (lane % HEAD_LANES) < DIFF_DK
    qo_ref[0] = jnp.where(first, q, 0.0).astype(BF16)
    qo_ref[1] = jnp.where(first, 0.0, q).astype(BF16)
    ko_ref[...] = norm_rope(k_ref[...], kg_ref[...]).astype(BF16)
    v = v_ref[...]
    ones_tile = jnp.where(lax.broadcasted_iota(jnp.int32, (BF16_SUBLANES, v.shape[0]), 0) == 0, 1.0, 0.0)
    for h in range(DIFF_HEADS):
        vt_ref[h, 0] = jnp.concatenate([v[:, h * DIFF_DV:(h + 1) * DIFF_DV].T, ones_tile], axis=0).astype(BF16)


def attn_prep(p_main, cos, sin, q_gain, k_gain):
    rows = p_main.shape[0]
    tm = ATTN_TK
    nb = rows // tm
    col = lambda name: MAIN_OFFSETS[name][0] // GROUP_WIDTH
    seg = lambda name: pl.BlockSpec((tm, GROUP_WIDTH), lambda i, c=col(name): (i, c))
    tab = pl.BlockSpec((tm, HEAD_LANES), lambda i: (i, 0))
    vec = pl.BlockSpec((1, GROUP_WIDTH), lambda i: (0, 0))
    return pl.pallas_call(
        _attn_prep_body,
        grid=(nb,),
        in_specs=[seg('diff_q'), seg('diff_k'), seg('diff_v'), tab, tab, vec, vec],
        out_specs=[
            pl.BlockSpec((2, tm, GROUP_WIDTH), lambda i: (0, i, 0)),
            pl.BlockSpec((tm, GROUP_WIDTH), lambda i: (i, 0)),
            pl.BlockSpec((DIFF_HEADS, 1, DV_EXT, tm), lambda i: (0, i, 0, 0)),
        ],
        out_shape=[
            jax.ShapeDtypeStruct((2, rows, GROUP_WIDTH), BF16),
            jax.ShapeDtypeStruct((rows, GROUP_WIDTH), BF16),
            jax.ShapeDtypeStruct((DIFF_HEADS, nb, DV_EXT, tm), BF16),
        ],
        compiler_params=pltpu.CompilerParams(dimension_semantics=("parallel",), vmem_limit_bytes=VMEM_LIMIT),
        name="attn_prep",
    )(p_main, p_main, p_main, cos, sin, q_gain, k_gain)


def _attn_body(lam_ref, q_ref, k_ref, vt_ref, gn_ref, o_ref, m_ref, acc_ref, st_ref, p_ref, *,
               n_lat_chunks, n_ctx, tq, post_scale):
    tk = ATTN_TK
    q2 = q_ref[...].reshape(2 * tq, HEAD_LANES)
    m_ref[...] = jnp.full(m_ref.shape, NEG_BIG, F32)
    acc_ref[...] = jnp.zeros(acc_ref.shape, F32)

    def scores(kc):
        return lax.dot_general(kc, q2, (((1,), (1,)), ((), ())), preferred_element_type=F32)

    def softmax_update(st, pv):
        m_prev = m_ref[...]
        m_new = jnp.maximum(m_prev, jnp.max(st, axis=0, keepdims=True))
        alpha = jnp.exp2(m_prev - m_new)
        acc_ref[...] = alpha * (acc_ref[...] + pv)
        m_ref[...] = m_new
        return jnp.exp2(st - m_new).astype(BF16)

    ctx0 = n_lat_chunks * tk
    p_ctx = softmax_update(scores(k_ref[ctx0:ctx0 + n_ctx, :]), 0.0)
    acc_ref[...] = acc_ref[...] + jnp.dot(vt_ref[n_lat_chunks][:, :n_ctx], p_ctx, preferred_element_type=F32)

    if n_lat_chunks:
        st_ref[0] = scores(k_ref[0:tk, :])
        p_ref[1] = jnp.zeros(p_ref.shape[1:], BF16)

        def half(j, cur, nxt):
            jn = jnp.minimum(j + 1, n_lat_chunks - 1)
            st_ref[nxt] = scores(k_ref[pl.ds(pl.multiple_of(jn * tk, tk), tk), :])
            pv = jnp.dot(vt_ref[jnp.maximum(j - 1, 0)], p_ref[nxt], preferred_element_type=F32)
            p_ref[cur] = softmax_update(st_ref[cur], pv)

        def body(i, carry):
            half(2 * i, 0, 1)
            half(2 * i + 1, 1, 0)
            return carry
        lax.fori_loop(0, n_lat_chunks // 2, body, 0)
        acc_ref[...] = acc_ref[...] + jnp.dot(vt_ref[n_lat_chunks - 1], p_ref[1], preferred_element_type=F32)

    acc = acc_ref[...]
    l = acc[DIFF_DV:DIFF_DV + 1]
    o = acc[:DIFF_DV, :tq] / l[:, :tq] - lam_ref[0, 0] * (acc[:DIFF_DV, tq:] / l[:, tq:])
    y = o * lax.rsqrt(jnp.mean(o * o, axis=0, keepdims=True) + NORM_EPS) * (gn_ref[...] * post_scale)
    o_ref[...] = y.T


def diff_attention(lam, q, k, vt, out_gain, post_scale, n_lat, n_ctx, context_queries):
    lat_chunks = n_lat // ATTN_TK
    if context_queries:
        rows, tq, q_off, n_lat_chunks = n_ctx, n_ctx, n_lat // n_ctx, 0
        k_spec = pl.BlockSpec((ATTN_TK, HEAD_LANES), lambda h, i: (lat_chunks, h))
        vt_spec = pl.BlockSpec((None, 1, DV_EXT, ATTN_TK), lambda h, i: (h, lat_chunks, 0, 0))
    else:
        assert lat_chunks % 2 == 0 and n_lat % ATTN_TQ == 0
        rows, tq, q_off, n_lat_chunks = n_lat, ATTN_TQ, 0, lat_chunks
        k_spec = pl.BlockSpec(((lat_chunks + 1) * ATTN_TK, HEAD_LANES), lambda h, i: (0, h))
        vt_spec = pl.BlockSpec((None, lat_chunks + 1, DV_EXT, ATTN_TK), lambda h, i: (h, 0, 0, 0))
    return pl.pallas_call(
        functools.partial(_attn_body, n_lat_chunks=n_lat_chunks, n_ctx=n_ctx, tq=tq, post_scale=post_scale),
        grid=(DIFF_HEADS, rows // tq),
        in_specs=[
            pl.BlockSpec(memory_space=pltpu.SMEM),
            pl.BlockSpec((2, tq, HEAD_LANES), lambda h, i: (0, q_off + i, h)),
            k_spec, vt_spec,
            pl.BlockSpec((DIFF_DV, 1), lambda h, i: (0, 0)),
        ],
        out_specs=pl.BlockSpec((tq, DIFF_DV), lambda h, i: (i, h)),
        out_shape=jax.ShapeDtypeStruct((rows, GROUP_WIDTH), F32),
        scratch_shapes=[
            pltpu.VMEM((1, 2 * tq), F32),
            pltpu.VMEM((DV_EXT, 2 * tq), F32),
            pltpu.VMEM((2, ATTN_TK, 2 * tq), F32),
            pltpu.VMEM((2, ATTN_TK, 2 * tq), BF16),
        ],
        compiler_params=pltpu.CompilerParams(
            dimension_semantics=("parallel", "arbitrary"), vmem_limit_bytes=VMEM_LIMIT),
        name="diff_attn_ctx" if context_queries else "diff_attn_lat",
    )(lam, q, k, vt, out_gain)


def rope_tables(n_lat, rows):
    pos = jnp.arange(n_lat, dtype=jnp.int32)
    row = (pos // GRID_W).astype(F32)
    col = (pos % GRID_W).astype(F32)
    inv_freq = ROPE_BASE ** (-jnp.arange(ROPE_PAIRS, dtype=F32) / ROPE_PAIRS)
    ang_r, ang_c = row[:, None] * inv_freq, col[:, None] * inv_freq
    cos = jnp.concatenate([jnp.cos(ang_r), jnp.cos(ang_r), jnp.cos(ang_c), jnp.cos(ang_c)], axis=-1)
    sin = jnp.concatenate([-jnp.sin(ang_r), jnp.sin(ang_r), -jnp.sin(ang_c), jnp.sin(ang_c)], axis=-1)
    cos, sin = jnp.tile(cos, (1, 2)), jnp.tile(sin, (1, 2))
    pad = ((0, rows - n_lat), (0, 0))
    return jnp.pad(cos, pad, constant_values=1.0), jnp.pad(sin, pad)


def diff_lambda_scalar(lam_params, lam_init):
    lq1, lk1, lq2, lk2 = lam_params
    return (jnp.exp(jnp.sum(lq1 * lk1)) - jnp.exp(jnp.sum(lq2 * lk2)) + lam_init).reshape(1, 1)


GDN_QKV_WIDTH = GDN_HEADS * (2 * GDN_DK + GDN_DV)
GDN_PREP_TM = 512
HALO = 8


def _softplus(z):
    return jnp.maximum(z, 0.0) + jnp.log(1.0 + jnp.exp(-jnp.abs(z)))


def _conv3(x, prev_row, next_row, w, row0, n_lat, n_tok):
    tm = x.shape[0]
    r = lax.broadcasted_iota(jnp.int32, (tm, 1), 0)
    rows = row0 + r
    xm1 = jnp.where(r == 0, prev_row, pltpu.roll(x, 1, 0))
    xm1 = jnp.where((rows == 0) | (rows == n_lat), 0.0, xm1)
    xp1 = jnp.where(r == tm - 1, next_row, pltpu.roll(x, tm - 1, 0))
    xp1 = jnp.where((rows == n_lat - 1) | (rows == n_tok - 1), 0.0, xp1)
    return xm1 * w[0:1] + x * w[1:2] + xp1 * w[2:3]


def _gdn_prep_body(x_ref, xp_ref, xn_ref, ps_ref, cw_ref, a_ref, dt_ref, q_ref, k_ref, v_ref, gs_ref, *, n_lat, n_tok):
    tm = x_ref.shape[0]
    y = _conv3(x_ref[...], xp_ref[HALO - 1:HALO, :], xn_ref[0:1, :], cw_ref[...], pl.program_id(0) * tm, n_lat, n_tok)
    y = y * jax.nn.sigmoid(y)
    hk = GDN_HEADS * GDN_DK
    q, k = y[:, :hk], y[:, hk:2 * hk]
    q_ref[...] = q * lax.rsqrt(_group_mean_sq(q, GDN_DK) * GDN_DK + NORM_EPS) * (GDN_DK ** -0.5)
    k_ref[...] = k * lax.rsqrt(_group_mean_sq(k, GDN_DK) * GDN_DK + NORM_EPS)
    v_ref[...] = y[:, 2 * hk:]
    ps = ps_ref[...]
    lane = lax.broadcasted_iota(jnp.int32, ps.shape, 1)
    gs_ref[...] = jnp.where(lane < 2 * GDN_HEADS, jax.nn.sigmoid(ps), a_ref[...] * _softplus(ps + dt_ref[...]))


def gdn_prep(p_main, p_small, conv_w, neg_a, dt, n_lat, n_tok):
    rows = p_main.shape[0]
    tm = GDN_PREP_TM
    nb = rows // tm
    per = tm // HALO
    out = jax.ShapeDtypeStruct((rows, GROUP_WIDTH), F32)
    return pl.pallas_call(
        functools.partial(_gdn_prep_body, n_lat=n_lat, n_tok=n_tok),
        grid=(nb,),
        in_specs=[
            pl.BlockSpec((tm, GDN_QKV_WIDTH), lambda i: (i, 0)),
            pl.BlockSpec((HALO, GDN_QKV_WIDTH), lambda i: (jnp.maximum(i * per - 1, 0), 0)),
            pl.BlockSpec((HALO, GDN_QKV_WIDTH), lambda i: (jnp.minimum((i + 1) * per, nb * per - 1), 0)),
            pl.BlockSpec((tm, SMALL_WIDTH), lambda i: (i, 0)),
            pl.BlockSpec((3, GDN_QKV_WIDTH), lambda i: (0, 0)),
            pl.BlockSpec((1, SMALL_WIDTH), lambda i: (0, 0)),
            pl.BlockSpec((1, SMALL_WIDTH), lambda i: (0, 0)),
        ],
        out_specs=[pl.BlockSpec((tm, GROUP_WIDTH), lambda i: (i, 0))] * 3 + [pl.BlockSpec((tm, SMALL_WIDTH), lambda i: (i, 0))],
        out_shape=[out, out, out, jax.ShapeDtypeStruct((rows, SMALL_WIDTH), F32)],
        compiler_params=pltpu.CompilerParams(dimension_semantics=("parallel",), vmem_limit_bytes=VMEM_LIMIT),
        name="gdn_prep",
    )(p_main, p_main, p_main, p_small, conv_w, neg_a, dt)


PAIR = 2 * SCAN_CHUNK


def _stack_pair(a):
    first = lax.broadcasted_iota(jnp.int32, a.shape, 1) < (PAIR // 2)
    return jnp.concatenate([jnp.where(first, a, 0.0), jnp.where(first, 0.0, a)], axis=0)


def _pair_col(arr, c0, c1):
    return jnp.concatenate([arr[:, c0:c0 + 1], arr[:, c1:c1 + 1]], axis=0)


def _dot_nt(a, b):
    return lax.dot_general(a, b, (((1,), (1,)), ((), ())), preferred_element_type=F32)


def _dot_tn(a, b):
    return lax.dot_general(a, b, (((0,), (0,)), ((), ())), preferred_element_type=F32)


def _chunk_masks(d):
    C = SCAN_CHUNK
    r = lax.broadcasted_iota(jnp.int32, (PAIR, PAIR), 0)
    c = lax.broadcasted_iota(jnp.int32, (PAIR, PAIR), 1)
    same = (r // C) == (c // C)
    ri, ci = r % C, c % C
    incl = same & ((ci <= ri) if d == 0 else (ci >= ri))
    strict = same & ((ci < ri) if d == 0 else (ci > ri))
    r1 = lax.broadcasted_iota(jnp.int32, (C, C), 0)
    c1 = lax.broadcasted_iota(jnp.int32, (C, C), 1)
    tri = jnp.where((c1 <= r1) if d == 0 else (c1 >= r1), 1.0, 0.0).astype(F32)
    return incl, strict, tri, r == c


def _gdn_scan_body(qf, kf, vf, gf, qb, kb, vb, gb, of_ref, ob_ref, s_ref):
    C = SCAN_CHUNK
    dot = functools.partial(jnp.dot, preferred_element_type=F32)

    @pl.when(pl.program_id(0) == 0)
    def _():
        s_ref[...] = jnp.zeros(s_ref.shape, F32)

    top = lax.broadcasted_iota(jnp.int32, (PAIR, 1), 0) < C
    units = []
    for d, (q_ref, k_ref, v_ref, g_ref, o_ref) in enumerate(((qf, kf, vf, gf, of_ref), (qb, kb, vb, gb, ob_ref))):
        incl, strict, tri, eye = _chunk_masks(d)
        gs = g_ref[...]
        g_all = jnp.dot(tri, gs, preferred_element_type=F32, precision=HI)
        g_end = g_all[C - 1:C] if d == 0 else g_all[0:1]
        g_t = g_all.T
        for p in range(GDN_HEADS // 2):
            sl = slice(PAIR * p, PAIR * (p + 1))
            b0 = GDN_HEADS * d + 2 * p
            c0 = 2 * GDN_HEADS + b0
            u = dict(d=d, p=p, sl=sl, o_ref=o_ref, incl=incl, strict=strict, eye=eye)
            u['K'], u['Q'], u['V'] = _stack_pair(k_ref[:, sl]), _stack_pair(q_ref[:, sl]), _stack_pair(v_ref[:, sl])
            u['g_col'] = _pair_col(g_all, c0, c0 + 1)
            u['beta'] = _pair_col(gs, b0, b0 + 1)
            u['g_row'] = jnp.concatenate([g_t[c0:c0 + 1, :], g_t[c0 + 1:c0 + 2, :]], axis=1)
            u['gl_col'] = jnp.where(top, g_end[:, c0:c0 + 1], g_end[:, c0 + 1:c0 + 2])
            units.append(u)
    for u in units:
        u['decay'] = jnp.exp(jnp.where(u['incl'], u['g_col'] - u['g_row'], NEG_BIG))
        u['eg'] = jnp.exp(u['g_col'])
    for u in units:
        u['A'] = jnp.where(u['strict'], -(u['beta'] * _dot_nt(u['K'], u['K']) * u['decay']), 0.0)
        u['P'] = jnp.where(u['eye'], 1.0, 0.0) + u['A']
        u['negL'] = u['A']
    for u in units:
        u['attn'] = _dot_nt(u['Q'], u['K']) * u['decay']
    for _ in range(5):
        for u in units:
            u['A'] = dot(u['A'], u['A'])
        for u in units:
            u['P'] = u['P'] + dot(u['P'], u['A'])
    for u in units:
        u['R'] = (jnp.where(u['eye'], 1.0, 0.0) - u['P']) + jnp.dot(u['negL'], u['P'], preferred_element_type=F32, precision=HI)
    for u in units:
        u['P'] = u['P'] + dot(u['P'], u['R'])
    for u in units:
        u['uw'] = dot(u['P'], jnp.concatenate([u['V'] * u['beta'], u['K'] * (u['beta'] * u['eg'])], axis=1))
    for u in units:
        u['S'] = s_ref[u['d'], u['p']]
        u['ws'] = dot(jnp.concatenate([u['uw'][:, PAIR:], u['Q'] * u['eg']], axis=0), u['S'])
    for u in units:
        u['v_new'] = u['uw'][:, :PAIR] - u['ws'][:PAIR]
        o_st = u['ws'][PAIR:] + dot(u['attn'], u['v_new'])
        u['o_ref'][:, u['sl']] = o_st[:C] + o_st[C:]
    for u in units:
        s_ref[u['d'], u['p']] = (u['S'] * jnp.exp(u['gl_col'])
                                 + _dot_tn(u['K'] * jnp.exp(u['gl_col'] - u['g_col']), u['v_new']))


def _scan_order(n_lat, n_ctx):
    nl, nc = n_lat // SCAN_CHUNK, n_ctx // SCAN_CHUNK
    fwd = lambda t: (jnp.where(t < nc, nl + t, t - nc), 0)
    bwd = lambda t: (nl + nc - 1 - t, 0)
    return nl + nc, fwd, bwd


def gdn_scan(q, k, v, gs, n_lat, n_ctx):
    steps, fwd, bwd = _scan_order(n_lat, n_ctx)
    wide = lambda f: pl.BlockSpec((SCAN_CHUNK, GROUP_WIDTH), f)
    small = lambda f: pl.BlockSpec((SCAN_CHUNK, SMALL_WIDTH), f)
    out = jax.ShapeDtypeStruct((n_lat + n_ctx, GROUP_WIDTH), F32)
    return pl.pallas_call(
        _gdn_scan_body,
        grid=(steps,),
        in_specs=[wide(fwd), wide(fwd), wide(fwd), small(fwd), wide(bwd), wide(bwd), wide(bwd), small(bwd)],
        out_specs=[wide(fwd), wide(bwd)],
        out_shape=[out, out],
        scratch_shapes=[pltpu.VMEM((2, GDN_HEADS // 2, PAIR, PAIR), F32)],
        compiler_params=pltpu.CompilerParams(dimension_semantics=("arbitrary",), vmem_limit_bytes=VMEM_LIMIT),
        name="gdn_scan",
    )(q, k, v, gs, q, k, v, gs)


def gdn_gate_vectors(a_log, dt_bias):
    n = 2 * GDN_HEADS
    pad = lambda t: jnp.pad(t.reshape(1, n), ((0, 0), (n, SMALL_WIDTH - 2 * n)))
    return pad(-jnp.exp(a_log)), pad(dt_bias)


GLA_QK_WIDTH = GLA_HEADS * GLA_DK


def _gla_scan_body(qf, kf, vf, pf, qb, kb, vb, pb, wup_ref, bias_ref, of_ref, ob_ref, s_ref):
    C = SCAN_CHUNK
    dot = functools.partial(jnp.dot, preferred_element_type=F32)

    @pl.when(pl.program_id(0) == 0)
    def _():
        s_ref[...] = jnp.zeros(s_ref.shape, F32)

    units = []
    for d, (q_ref, k_ref, v_ref, p_ref, o_ref) in enumerate(((qf, kf, vf, pf, of_ref), (qb, kb, vb, pb, ob_ref))):
        incl, _, tri, _ = _chunk_masks(d)
        z = dot(p_ref[...], wup_ref[...]) + bias_ref[...]
        z = z[:, GLA_QK_WIDTH * d:GLA_QK_WIDTH * (d + 1)]
        log_a = (jnp.minimum(z, 0.0) - jnp.log(1.0 + jnp.exp(-jnp.abs(z)))) * (1.0 / GLA_TAU)
        b = jnp.dot(tri, log_a, preferred_element_type=F32, precision=HI)
        b_end = b[C - 1:C] if d == 0 else b[0:1]
        q = q_ref[...] * (GLA_DK ** -0.5)
        k_out = k_ref[...] * jnp.exp(b_end - b)
        q_in = q * jnp.exp(b - b_end)
        q_st = q * jnp.exp(b)
        a_col = jnp.broadcast_to(jnp.exp(b_end), (C, GLA_QK_WIDTH)).T[:, 0:1]
        v = v_ref[...]
        for p in range(GLA_HEADS // 2):
            sl = slice(PAIR * p, PAIR * (p + 1))
            v0, v1 = v[:, 2 * p * GLA_DV:(2 * p + 1) * GLA_DV], v[:, (2 * p + 1) * GLA_DV:(2 * p + 2) * GLA_DV]
            units.append(dict(d=d, p=p, o_ref=o_ref, incl=incl, V=jnp.concatenate([v0, v1], axis=0),
                              K=_stack_pair(k_out[:, sl]), Qin=_stack_pair(q_in[:, sl]), Qst=_stack_pair(q_st[:, sl]),
                              a=a_col[sl]))
    for u in units:
        u['attn'] = jnp.where(u['incl'], _dot_nt(u['Qin'], u['K']), 0.0)
        u['S'] = s_ref[u['d'], u['p']]
    for u in units:
        u['o'] = dot(u['Qst'], u['S'])
        u['dS'] = _dot_tn(u['K'], u['V'])
    for u in units:
        o_st = u['o'] + dot(u['attn'], u['V'])
        p = u['p']
        u['o_ref'][:, 2 * p * GLA_DV:(2 * p + 1) * GLA_DV] = o_st[:C]
        u['o_ref'][:, (2 * p + 1) * GLA_DV:(2 * p + 2) * GLA_DV] = o_st[C:]
        s_ref[u['d'], p] = u['a'] * u['S'] + u['dS']


def gla_scan(p_main, p_small, w_up, bias, n_lat, n_ctx):
    steps, fwd, bwd = _scan_order(n_lat, n_ctx)
    qc, kc, vc = (MAIN_OFFSETS[n][0] for n in ('gla_q', 'gla_k', 'gla_v'))
    at = lambda f, blk: (lambda t: (f(t)[0], blk))
    qk = lambda f, off: pl.BlockSpec((SCAN_CHUNK, GLA_QK_WIDTH), at(f, off // GLA_QK_WIDTH))
    vv = lambda f: pl.BlockSpec((SCAN_CHUNK, GROUP_WIDTH), at(f, vc // GROUP_WIDTH))
    small = lambda f: pl.BlockSpec((SCAN_CHUNK, SMALL_WIDTH), f)
    const = lambda shape: pl.BlockSpec(shape, lambda t: (0, 0))
    out = jax.ShapeDtypeStruct((n_lat + n_ctx, GROUP_WIDTH), F32)
    return pl.pallas_call(
        _gla_scan_body,
        grid=(steps,),
        in_specs=[qk(fwd, qc), qk(fwd, kc), vv(fwd), small(fwd), qk(bwd, qc), qk(bwd, kc), vv(bwd), small(bwd),
                  const((SMALL_WIDTH, 2 * GLA_QK_WIDTH)), const((1, 2 * GLA_QK_WIDTH))],
        out_specs=[pl.BlockSpec((SCAN_CHUNK, GROUP_WIDTH), fwd), pl.BlockSpec((SCAN_CHUNK, GROUP_WIDTH), bwd)],
        out_shape=[out, out],
        scratch_shapes=[pltpu.VMEM((2, GLA_HEADS // 2, PAIR, GLA_DV), F32)],
        compiler_params=pltpu.CompilerParams(dimension_semantics=("arbitrary",), vmem_limit_bytes=VMEM_LIMIT),
        name="gla_scan",
    )(p_main, p_main, p_main, p_small, p_main, p_main, p_main, p_small, w_up, bias)


def gla_gate_weights(gate_up, gate_bias):
    off = SMALL_OFFSETS['gla_lowrank'][0]
    w = jnp.zeros((SMALL_WIDTH, 2 * GLA_QK_WIDTH), F32)
    for d in range(2):
        w = w.at[off + d * GLA_GATE_RANK:off + (d + 1) * GLA_GATE_RANK, d * GLA_QK_WIDTH:(d + 1) * GLA_QK_WIDTH].set(gate_up[d])
    return w, gate_bias.reshape(1, 2 * GLA_QK_WIDTH)


MIX_TM = 256
ROUTER_WIDTH = LANE


def _mixproj_body(x_ref, gf_ref, gb_ref, gate_ref, scb_ref, scc_ref, scx_ref, ccp_ref, ccn_ref, cxp_ref, cxn_ref,
                  lf_ref, lb_ref, r_ref, dl_ref, dc_ref, w_ref, g2_ref, gng_ref, gnl_ref, scw_ref, n2_ref,
                  shift_ref, scale_ref, wr_ref, xo_ref, h2_ref, lg_ref, *, n_lat, n_tok):
    tm = x_ref.shape[0]
    row0 = pl.program_id(0) * tm
    silu = lambda t: t * jax.nn.sigmoid(t)
    dot = functools.partial(jnp.dot, preferred_element_type=F32)

    og = gf_ref[...] + gb_ref[...]
    a = og * lax.rsqrt(_group_mean_sq(og, GDN_DV) + NORM_EPS) * gng_ref[...] * silu(gate_ref[...])
    u = scc_ref[...] * scx_ref[...]
    u_prev = ccp_ref[HALO - 1:HALO, :] * cxp_ref[HALO - 1:HALO, :]
    u_next = ccn_ref[0:1, :] * cxn_ref[0:1, :]
    b = scb_ref[...] * _conv3(u, u_prev, u_next, scw_ref[...], row0, n_lat, n_tok)
    ol = lf_ref[...] + lb_ref[...]
    g = ol * lax.rsqrt(_group_mean_sq(ol, GLA_DV) + NORM_EPS) * gnl_ref[...] * silu(r_ref[...])
    d = jnp.where(row0 < n_lat, dl_ref[...], dc_ref[...])

    W = GROUP_WIDTH
    y = (dot(a.astype(BF16), w_ref[0:W, :]) + dot(b.astype(BF16), w_ref[W:2 * W, :])
         + dot(g.astype(BF16), w_ref[2 * W:3 * W, :]) + dot(d.astype(BF16), w_ref[3 * W:4 * W, :]))
    xn = x_ref[...] + g2_ref[...] * y
    xo_ref[...] = xn
    h2 = xn * lax.rsqrt(jnp.mean(xn * xn, axis=-1, keepdims=True) + NORM_EPS) * n2_ref[...]
    h2 = h2 * (1.0 + scale_ref[...]) + shift_ref[...]
    h2_ref[...] = h2.astype(BF16)
    lg_ref[...] = jnp.dot(h2, wr_ref[...], preferred_element_type=F32, precision=HI)


def mixproj(xa, pm, gdn_f, gdn_b, gla_f, gla_b, d_lat, d_ctx, w_out, gate, gdn_gain, gla_gain, sc_w, norm2,
            shift, scale, w_router, n_lat, n_ctx):
    rows = xa.shape[0]
    n_tok = n_lat + n_ctx
    tm = MIX_TM
    assert n_lat % tm == 0 and n_ctx == tm
    lat_tiles = n_lat // tm
    per = tm // HALO
    last_halo = rows // HALO - 1
    seg = lambda name: pl.BlockSpec((tm, GROUP_WIDTH), lambda i, c=MAIN_OFFSETS[name][0] // GROUP_WIDTH: (i, c))
    halo_prev = lambda name: pl.BlockSpec(
        (HALO, GROUP_WIDTH), lambda i, c=MAIN_OFFSETS[name][0] // GROUP_WIDTH: (jnp.maximum(i * per - 1, 0), c))
    halo_next = lambda name: pl.BlockSpec(
        (HALO, GROUP_WIDTH), lambda i, c=MAIN_OFFSETS[name][0] // GROUP_WIDTH: (jnp.minimum((i + 1) * per, last_halo), c))
    tok = pl.BlockSpec((tm, GROUP_WIDTH), lambda i: (i, 0))
    full = pl.BlockSpec((tm, D_MODEL), lambda i: (i, 0))
    mod = pl.BlockSpec((None, 1, D_MODEL), lambda i: (i // lat_tiles, 0, 0))
    const = lambda shape: pl.BlockSpec(shape, lambda i: (0,) * len(shape))
    return pl.pallas_call(
        functools.partial(_mixproj_body, n_lat=n_lat, n_tok=n_tok),
        grid=(n_tok // tm,),
        in_specs=[
            full, tok, tok, seg('gdn_gate'), seg('sc_b'), seg('sc_c'), seg('sc_x'),
            halo_prev('sc_c'), halo_next('sc_c'), halo_prev('sc_x'), halo_next('sc_x'),
            tok, tok, seg('gla_r'),
            pl.BlockSpec((tm, GROUP_WIDTH), lambda i: (jnp.minimum(i, lat_tiles - 1), 0)),
            pl.BlockSpec((tm, GROUP_WIDTH), lambda i: (0, 0)),
            const((D_MODEL, D_MODEL)), mod, const((1, GROUP_WIDTH)), const((1, GROUP_WIDTH)), const((3, GROUP_WIDTH)),
            const((1, D_MODEL)), mod, mod, const((D_MODEL, ROUTER_WIDTH)),
        ],
        out_specs=[full, full, pl.BlockSpec((tm, ROUTER_WIDTH), lambda i: (i, 0))],
        out_shape=[
            jax.ShapeDtypeStruct((rows, D_MODEL), F32),
            jax.ShapeDtypeStruct((n_tok, D_MODEL), BF16),
            jax.ShapeDtypeStruct((n_tok, ROUTER_WIDTH), F32),
        ],
        input_output_aliases={0: 0},
        compiler_params=pltpu.CompilerParams(dimension_semantics=("parallel",), vmem_limit_bytes=VMEM_LIMIT),
        name="mixproj",
    )(xa, gdn_f, gdn_b, pm, pm, pm, pm, pm, pm, pm, pm, gla_f, gla_b, pm, d_lat, d_ctx, w_out, gate,
      gdn_gain, gla_gain, sc_w, norm2, shift, scale, w_router)


def _ffn_body(be_ref, nu_ref, x_ref, w1_ref, w3_ref, w2_ref, o_ref, w1b_ref, w3b_ref, w2b_ref):
    b = pl.program_id(0)
    used = b < nu_ref[0]
    new_expert = jnp.logical_or(b == 0, be_ref[b] != be_ref[jnp.maximum(b - 1, 0)])

    @pl.when(jnp.logical_and(used, new_expert))
    def _():
        w1b_ref[...] = w1_ref[...].astype(BF16)
        w3b_ref[...] = w3_ref[...].astype(BF16)
        w2b_ref[...] = w2_ref[...].astype(BF16)

    @pl.when(used)
    def _():
        x = x_ref[...]
        a = jnp.dot(x, w1b_ref[...], preferred_element_type=F32)
        g = jnp.dot(x, w3b_ref[...], preferred_element_type=F32)
        mid = (a * jax.nn.sigmoid(a) * g).astype(BF16)
        o_ref[...] = jnp.dot(mid, w2b_ref[...], preferred_element_type=F32)

    @pl.when(jnp.logical_not(used))
    def _():
        o_ref[...] = jnp.zeros(o_ref.shape, F32)


def expert_ffn(block_expert, n_used, xs, w1, w3, w2):
    L = xs.shape[0]
    n_blocks = L // EXPERT_BLOCK
    wspec = lambda shape: pl.BlockSpec((None,) + shape, lambda b, be, nu: (be[b], 0, 0))
    return pl.pallas_call(
        _ffn_body,
        grid_spec=pltpu.PrefetchScalarGridSpec(
            num_scalar_prefetch=2,
            grid=(n_blocks,),
            in_specs=[
                pl.BlockSpec((EXPERT_BLOCK, D_MODEL), lambda b, be, nu: (b, 0)),
                wspec((D_MODEL, D_EXPERT)), wspec((D_MODEL, D_EXPERT)), wspec((D_EXPERT, D_MODEL)),
            ],
            out_specs=pl.BlockSpec((EXPERT_BLOCK, D_MODEL), lambda b, be, nu: (b, 0)),
            scratch_shapes=[
                pltpu.VMEM((D_MODEL, D_EXPERT), BF16),
                pltpu.VMEM((D_MODEL, D_EXPERT), BF16),
                pltpu.VMEM((D_EXPERT, D_MODEL), BF16),
            ],
        ),
        out_shape=jax.ShapeDtypeStruct((L, D_MODEL), F32),
        compiler_params=pltpu.CompilerParams(dimension_semantics=("arbitrary",), vmem_limit_bytes=VMEM_LIMIT),
        name="expert_ffn",
    )(block_expert, n_used, xs, w1, w3, w2)


def hier_moe_pallas(hb, logits, bg, be, w1, w3, w2):
    N, D = hb.shape
    g_prob = jax.nn.softmax(logits[:, :N_GROUPS] + bg, axis=-1)
    g_top_p, g_idx = lax.top_k(g_prob, 1)
    e_logits = (logits[:, N_GROUPS:N_GROUPS + N_EXPERTS] + be).reshape(N, N_GROUPS, EXPERTS_PER_GROUP)
    e_in_group = jnp.take_along_axis(e_logits, g_idx[:, :, None], axis=1)[:, 0]
    e_top_p, e_top_i = lax.top_k(jax.nn.softmax(e_in_group, axis=-1), TOP_K)
    e_top_p = e_top_p / jnp.sum(e_top_p, axis=-1, keepdims=True)
    weights = g_top_p * e_top_p
    expert_idx = g_idx * EXPERTS_PER_GROUP + e_top_i

    E, M, K = N_EXPERTS, EXPERT_BLOCK, TOP_K
    A = N * K
    flat_e = expert_idx.reshape(-1).astype(jnp.int32)
    order = jnp.argsort(flat_e)
    e_s = flat_e[order]
    t_s = (order // K).astype(jnp.int32)
    counts = jnp.bincount(flat_e, length=E).astype(jnp.int32)
    padded = (counts + M - 1) // M * M
    pad_end = jnp.cumsum(padded)
    pad_start = pad_end - padded
    start = jnp.cumsum(counts) - counts
    dest = pad_start[e_s] + (jnp.arange(A, dtype=jnp.int32) - start[e_s])
    n_blocks = -(-A // M) + E
    L = n_blocks * M
    slot_tok = jnp.full((L,), N, jnp.int32).at[dest].set(t_s)
    block_expert = jnp.minimum(
        jnp.searchsorted(pad_end, jnp.arange(n_blocks, dtype=jnp.int32) * M, side='right'), E - 1).astype(jnp.int32)
    n_used = (pad_end[-1] // M).astype(jnp.int32).reshape(1)
    h_pad = jnp.concatenate([hb, jnp.zeros((1, D), BF16)], axis=0)
    xs = h_pad[slot_tok]
    y = expert_ffn(block_expert, n_used, xs, w1, w3, w2)
    pos = jnp.zeros((A,), jnp.int32).at[order].set(dest).reshape(N, K)
    return y[pos[:, 0]] * weights[:, 0:1] + y[pos[:, 1]] * weights[:, 1:2]


def _prep_w_in(w):
    cols = lambda names: jnp.concatenate(
        [w[:, PROJ_OFFSETS[n][0]:PROJ_OFFSETS[n][0] + PROJ_OFFSETS[n][1]] for n in names], axis=1)
    small = cols(SMALL_GROUPS)
    small = jnp.pad(small, ((0, 0), (0, SMALL_WIDTH - small.shape[1])))
    return cols(MAIN_GROUPS).astype(BF16), small.astype(BF16)


def kernel(x, c, ctx, c_ctx, w_mod, b_mod, norm1, norm2, w_in, w_out, gdn_conv, gdn_a_log, gdn_dt_bias, gdn_out_norm, sc_conv, gla_gate_up, gla_gate_bias, gla_out_norm, diff_q_norm, diff_k_norm, diff_lambda, diff_out_norm, router_group, router_group_bias, router_expert, router_expert_bias, expert_w1, expert_w3, expert_w2):
    n_lat, n_ctx = x.shape[1], ctx.shape[1]
    n_tok = n_lat + n_ctx
    rows = -(-n_tok // ROW_TILE) * ROW_TILE
    pad_rows = lambda t: jnp.pad(t, ((0, rows - t.shape[0]), (0, 0)))
    cos, sin = rope_tables(n_lat, rows)
    is_lat = jnp.arange(rows, dtype=jnp.int32)[:, None] < n_lat

    xa = pad_rows(jnp.concatenate([x[0], ctx[0]], axis=0))
    cond, cond_ctx = jax.nn.silu(c), jax.nn.silu(c_ctx)
    for l in range(DEPTH):
        with_ctx_out = l < DEPTH - 1
        lam_init = 0.8 - 0.6 * math.exp(-0.3 * l)
        m = jnp.split(cond @ w_mod[l] + b_mod[l], 6, axis=-1)
        mc = jnp.split((cond_ctx @ w_mod[l] + b_mod[l])[None, :], 6, axis=-1)
        mod = [jnp.stack([m[i], mc[i]], axis=0) for i in range(6)]

        w_main, w_small = _prep_w_in(w_in[l])
        pm, ps = inproj(xa, norm1[l][None, :], mod[0], mod[1], w_main, w_small, n_lat)

        neg_a, dt = gdn_gate_vectors(gdn_a_log[l], gdn_dt_bias[l])
        gq, gk, gv, gs = gdn_prep(pm, ps, gdn_conv[l], neg_a, dt, n_lat, n_tok)
        gdn_f, gdn_b = gdn_scan(gq, gk, gv, gs, n_lat, n_ctx)

        w_up, up_bias = gla_gate_weights(gla_gate_up[l], gla_gate_bias[l])
        gla_f, gla_b = gla_scan(pm, ps, w_up, up_bias, n_lat, n_ctx)

        lam = diff_lambda_scalar(diff_lambda[l], lam_init)
        q_gain = jnp.tile(diff_q_norm[l], 2 * DIFF_HEADS)[None, :]
        k_gain = jnp.tile(diff_k_norm[l], 2 * DIFF_HEADS)[None, :]
        d_gain = diff_out_norm[l][:, None]
        dq, dk, dvt = attn_prep(pm, cos, sin, q_gain, k_gain)
        d_lat = diff_attention(lam, dq, dk, dvt, d_gain, 1.0 - lam_init, n_lat, n_ctx, False)
        if with_ctx_out:
            d_ctx = diff_attention(lam, dq, dk, dvt, d_gain, 1.0 - lam_init, n_lat, n_ctx, True)
        else:
            d_ctx = jnp.zeros((n_ctx, GROUP_WIDTH), F32)

        w_router = jnp.pad(jnp.concatenate([router_group[l], router_expert[l]], axis=1),
                           ((0, 0), (0, ROUTER_WIDTH - N_GROUPS - N_EXPERTS)))
        xa, h2, logits = mixproj(
            xa, pm, gdn_f, gdn_b, gla_f, gla_b, d_lat, d_ctx, w_out[l].astype(BF16), mod[2],
            jnp.tile(gdn_out_norm[l], GDN_HEADS)[None, :], jnp.tile(gla_out_norm[l], GLA_HEADS)[None, :],
            sc_conv[l], norm2[l][None, :], mod[3], mod[4], w_router, n_lat, n_ctx)

        n_moe = n_tok if with_ctx_out else n_lat
        y = hier_moe_pallas(h2[:n_moe], logits[:n_moe], router_group_bias[l], router_expert_bias[l],
                            expert_w1[l], expert_w3[l], expert_w2[l])
        xa = xa + jnp.where(is_lat, m[5], mc[5]) * pad_rows(y)
    return xa[:n_lat][None]
```

```python
import functools
import math

import jax
import jax.numpy as jnp
from jax import lax
from jax.experimental import pallas as pl
from jax.experimental.pallas import tpu as pltpu

D_MODEL = 2048
DEPTH = 2
GRID_W = 64
GROUP_WIDTH = 512
GDN_DK = 64
GDN_DV = 64
GDN_HEADS = GROUP_WIDTH // GDN_DV
SCAN_CHUNK = 64
SC_WIDTH = GROUP_WIDTH
GLA_DK = 64
GLA_DV = 128
GLA_HEADS = GROUP_WIDTH // GLA_DV
GLA_GATE_RANK = 16
GLA_TAU = 16.0
DIFF_DK = 64
DIFF_DV = 128
DIFF_HEADS = GROUP_WIDTH // DIFF_DV
ROPE_BASE = 10000.0
ROPE_PAIRS = DIFF_DK // 4
N_GROUPS = 4
EXPERTS_PER_GROUP = 8
N_EXPERTS = N_GROUPS * EXPERTS_PER_GROUP
TOP_K = 2
D_EXPERT = D_MODEL // 4
EXPERT_BLOCK = 256
NORM_EPS = 1e-6
F32 = jnp.float32
BF16 = jnp.bfloat16
HI = lax.Precision.HIGHEST

PROJ_LAYOUT = (
    ('gdn_qkv', GDN_HEADS * (2 * GDN_DK + GDN_DV)),
    ('gdn_gate', GDN_HEADS * GDN_DV),
    ('gdn_beta', 2 * GDN_HEADS),
    ('gdn_alpha', 2 * GDN_HEADS),
    ('sc_b', SC_WIDTH),
    ('sc_c', SC_WIDTH),
    ('sc_x', SC_WIDTH),
    ('gla_q', GLA_HEADS * GLA_DK),
    ('gla_k', GLA_HEADS * GLA_DK),
    ('gla_v', GLA_HEADS * GLA_DV),
    ('gla_r', GLA_HEADS * GLA_DV),
    ('gla_lowrank', 2 * GLA_GATE_RANK),
    ('diff_q', DIFF_HEADS * 2 * DIFF_DK),
    ('diff_k', DIFF_HEADS * 2 * DIFF_DK),
    ('diff_v', DIFF_HEADS * DIFF_DV),
)
PROJ_OFFSETS = {}
_off = 0
for _name, _size in PROJ_LAYOUT:
    PROJ_OFFSETS[_name] = (_off, _size)
    _off += _size

MAIN_GROUPS = ('gdn_qkv', 'gdn_gate', 'sc_b', 'sc_c', 'sc_x', 'gla_q', 'gla_k', 'gla_v', 'gla_r',
               'diff_q', 'diff_k', 'diff_v')
SMALL_GROUPS = ('gdn_beta', 'gdn_alpha', 'gla_lowrank')
MAIN_OFFSETS = {}
_off = 0
for _name in MAIN_GROUPS:
    MAIN_OFFSETS[_name] = (_off, PROJ_OFFSETS[_name][1])
    _off += PROJ_OFFSETS[_name][1]
MAIN_WIDTH = _off
SMALL_OFFSETS = {}
_off = 0
for _name in SMALL_GROUPS:
    SMALL_OFFSETS[_name] = (_off, PROJ_OFFSETS[_name][1])
    _off += PROJ_OFFSETS[_name][1]
LANE = 128
SMALL_WIDTH = LANE

VMEM_LIMIT = 48 * 1024 * 1024
ROW_TILE = 1024


INPROJ_TM = 1024
INPROJ_TN = 512


def _inproj_body(x_ref, gain_ref, shift_ref, scale_ref, w_ref, ws_ref, o_ref, os_ref, h_ref):
    @pl.when(pl.program_id(1) == 0)
    def _():
        x = x_ref[...]
        y = x * lax.rsqrt(jnp.mean(x * x, axis=-1, keepdims=True) + NORM_EPS) * gain_ref[...]
        hb = (y * (1.0 + scale_ref[...]) + shift_ref[...]).astype(BF16)
        h_ref[...] = hb
        os_ref[...] = jnp.dot(hb, ws_ref[...], preferred_element_type=F32)

    o_ref[...] = jnp.dot(h_ref[...], w_ref[...], preferred_element_type=F32)


def inproj(x, gain, shift, scale, w_main, w_small, n_lat):
    rows = x.shape[0]
    tm = INPROJ_TM
    lat_tiles = n_lat // tm
    mod = pl.BlockSpec((None, 1, D_MODEL), lambda i, j: (i // lat_tiles, 0, 0))
    return pl.pallas_call(
        _inproj_body,
        grid=(rows // tm, MAIN_WIDTH // INPROJ_TN),
        in_specs=[
            pl.BlockSpec((tm, D_MODEL), lambda i, j: (i, 0)),
            pl.BlockSpec((1, D_MODEL), lambda i, j: (0, 0)),
            mod, mod,
            pl.BlockSpec((D_MODEL, INPROJ_TN), lambda i, j: (0, j)),
            pl.BlockSpec((D_MODEL, SMALL_WIDTH), lambda i, j: (0, 0)),
        ],
        out_specs=[
            pl.BlockSpec((tm, INPROJ_TN), lambda i, j: (i, j)),
            pl.BlockSpec((tm, SMALL_WIDTH), lambda i, j: (i, 0)),
        ],
        out_shape=[
            jax.ShapeDtypeStruct((rows, MAIN_WIDTH), F32),
            jax.ShapeDtypeStruct((rows, SMALL_WIDTH), F32),
        ],
        scratch_shapes=[pltpu.VMEM((tm, D_MODEL), BF16)],
        compiler_params=pltpu.CompilerParams(
            dimension_semantics=("parallel", "arbitrary"), vmem_limit_bytes=VMEM_LIMIT),
        name="inproj",
    )(x, gain, shift, scale, w_main, w_small)


HEAD_LANES = 2 * DIFF_DK
ATTN_TK = 256
ATTN_TQ = 1024
BF16_SUBLANES = 16
DV_EXT = DIFF_DV + BF16_SUBLANES
LOG2E = math.log2(math.e)
NEG_BIG = -1e30


def _group_mean_sq(x, width):
    n = x.shape[-1]
    gi = lax.broadcasted_iota(jnp.int32, (n, n), 0) // width
    gj = lax.broadcasted_iota(jnp.int32, (n, n), 1) // width
    ones = jnp.where(gi == gj, 1.0, 0.0).astype(BF16)
    rest = x * x
    total = None
    for _ in range(3):
        term = rest.astype(BF16)
        part = jnp.dot(term, ones, preferred_element_type=F32)
        total = part if total is None else total + part
        rest = rest - term.astype(F32)
    return total * (1.0 / width)


def _rope_swap(x):
    n = x.shape[-1]
    lane = lax.broadcasted_iota(jnp.int32, x.shape, x.ndim - 1)
    return jnp.where((lane % (2 * ROPE_PAIRS)) < ROPE_PAIRS,
                     pltpu.roll(x, n - ROPE_PAIRS, 1), pltpu.roll(x, ROPE_PAIRS, 1))


def _attn_prep_body(q_ref, k_ref, v_ref, cos_ref, sin_ref, qg_ref, kg_ref, qo_ref, ko_ref, vt_ref):
    cos = jnp.concatenate([cos_ref[...]] * DIFF_HEADS, axis=-1)
    sin = jnp.concatenate([sin_ref[...]] * DIFF_HEADS, axis=-1)

    def norm_rope(x, gain):
        y = x * lax.rsqrt(_group_mean_sq(x, DIFF_DK) + NORM_EPS) * gain
        return y * cos + _rope_swap(y) * sin

    q = norm_rope(q_ref[...], qg_ref[...]) * (DIFF_DK ** -0.5 * LOG2E)
    lane = lax.broadcasted_iota(jnp.int32, q.shape, 1)
    first = (lane % HEAD_LANES) < DIFF_DK
    qo_ref[0] = jnp.where(first, q, 0.0).astype(BF16)
    qo_ref[1] = jnp.where(first, 0.0, q).astype(BF16)
    ko_ref[...] = norm_rope(k_ref[...], kg_ref[...]).astype(BF16)
    v = v_ref[...]
    ones_tile = jnp.where(lax.broadcasted_iota(jnp.int32, (BF16_SUBLANES, v.shape[0]), 0) == 0, 1.0, 0.0)
    for h in range(DIFF_HEADS):
        vt_ref[h, 0] = jnp.concatenate([v[:, h * DIFF_DV:(h + 1) * DIFF_DV].T, ones_tile], axis=0).astype(BF16)


def attn_prep(p_main, cos, sin, q_gain, k_gain):
    rows = p_main.shape[0]
    tm = ATTN_TK
    nb = rows // tm
    col = lambda name: MAIN_OFFSETS[name][0] // GROUP_WIDTH
    seg = lambda name: pl.BlockSpec((tm, GROUP_WIDTH), lambda i, c=col(name): (i, c))
    tab = pl.BlockSpec((tm, HEAD_LANES), lambda i: (i, 0))
    vec = pl.BlockSpec((1, GROUP_WIDTH), lambda i: (0, 0))
    return pl.pallas_call(
        _attn_prep_body,
        grid=(nb,),
        in_specs=[seg('diff_q'), seg('diff_k'), seg('diff_v'), tab, tab, vec, vec],
        out_specs=[
            pl.BlockSpec((2, tm, GROUP_WIDTH), lambda i: (0, i, 0)),
            pl.BlockSpec((tm, GROUP_WIDTH), lambda i: (i, 0)),
            pl.BlockSpec((DIFF_HEADS, 1, DV_EXT, tm), lambda i: (0, i, 0, 0)),
        ],
        out_shape=[
            jax.ShapeDtypeStruct((2, rows, GROUP_WIDTH), BF16),
            jax.ShapeDtypeStruct((rows, GROUP_WIDTH), BF16),
            jax.ShapeDtypeStruct((DIFF_HEADS, nb, DV_EXT, tm), BF16),
        ],
        compiler_params=pltpu.CompilerParams(dimension_semantics=("parallel",), vmem_limit_bytes=VMEM_LIMIT),
        name="attn_prep",
    )(p_main, p_main, p_main, cos, sin, q_gain, k_gain)


def _attn_body(lam_ref, q_ref, k_ref, vt_ref, gn_ref, o_ref, m_ref, acc_ref, st_ref, p_ref, *,
               n_lat_chunks, n_ctx, tq, post_scale):
    tk = ATTN_TK
    q2 = q_ref[...].reshape(2 * tq, HEAD_LANES)
    m_ref[...] = jnp.full(m_ref.shape, NEG_BIG, F32)
    acc_ref[...] = jnp.zeros(acc_ref.shape, F32)

    def scores(kc):
        return lax.dot_general(kc, q2, (((1,), (1,)), ((), ())), preferred_element_type=F32)

    def softmax_update(st, pv):
        m_prev = m_ref[...]
        m_new = jnp.maximum(m_prev, jnp.max(st, axis=0, keepdims=True))
        alpha = jnp.exp2(m_prev - m_new)
        acc_ref[...] = alpha * (acc_ref[...] + pv)
        m_ref[...] = m_new
        return jnp.exp2(st - m_new).astype(BF16)

    ctx0 = n_lat_chunks * tk
    p_ctx = softmax_update(scores(k_ref[ctx0:ctx0 + n_ctx, :]), 0.0)
    acc_ref[...] = acc_ref[...] + jnp.dot(vt_ref[n_lat_chunks][:, :n_ctx], p_ctx, preferred_element_type=F32)

    if n_lat_chunks:
        st_ref[0] = scores(k_ref[0:tk, :])
        p_ref[1] = jnp.zeros(p_ref.shape[1:], BF16)

        def half(j, cur, nxt):
            jn = jnp.minimum(j + 1, n_lat_chunks - 1)
            st_ref[nxt] = scores(k_ref[pl.ds(pl.multiple_of(jn * tk, tk), tk), :])
            pv = jnp.dot(vt_ref[jnp.maximum(j - 1, 0)], p_ref[nxt], preferred_element_type=F32)
            p_ref[cur] = softmax_update(st_ref[cur], pv)

        def body(i, carry):
            half(2 * i, 0, 1)
            half(2 * i + 1, 1, 0)
            return carry
        lax.fori_loop(0, n_lat_chunks // 2, body, 0)
        acc_ref[...] = acc_ref[...] + jnp.dot(vt_ref[n_lat_chunks - 1], p_ref[1], preferred_element_type=F32)

    acc = acc_ref[...]
    l = acc[DIFF_DV:DIFF_DV + 1]
    o = acc[:DIFF_DV, :tq] / l[:, :tq] - lam_ref[0, 0] * (acc[:DIFF_DV, tq:] / l[:, tq:])
    y = o * lax.rsqrt(jnp.mean(o * o, axis=0, keepdims=True) + NORM_EPS) * (gn_ref[...] * post_scale)
    o_ref[...] = y.T


def diff_attention(lam, q, k, vt, out_gain, post_scale, n_lat, n_ctx, context_queries):
    lat_chunks = n_lat // ATTN_TK
    if context_queries:
        rows, tq, q_off, n_lat_chunks = n_ctx, n_ctx, n_lat // n_ctx, 0
        k_spec = pl.BlockSpec((ATTN_TK, HEAD_LANES), lambda h, i: (lat_chunks, h))
        vt_spec = pl.BlockSpec((None, 1, DV_EXT, ATTN_TK), lambda h, i: (h, lat_chunks, 0, 0))
    else:
        assert lat_chunks % 2 == 0 and n_lat % ATTN_TQ == 0
        rows, tq, q_off, n_lat_chunks = n_lat, ATTN_TQ, 0, lat_chunks
        k_spec = pl.BlockSpec(((lat_chunks + 1) * ATTN_TK, HEAD_LANES), lambda h, i: (0, h))
        vt_spec = pl.BlockSpec((None, lat_chunks + 1, DV_EXT, ATTN_TK), lambda h, i: (h, 0, 0, 0))
    return pl.pallas_call(
        functools.partial(_attn_body, n_lat_chunks=n_lat_chunks, n_ctx=n_ctx, tq=tq, post_scale=post_scale),
        grid=(DIFF_HEADS, rows // tq),
        in_specs=[
            pl.BlockSpec(memory_space=pltpu.SMEM),
            pl.BlockSpec((2, tq, HEAD_LANES), lambda h, i: (0, q_off + i, h)),
            k_spec, vt_spec,
            pl.BlockSpec((DIFF_DV, 1), lambda h, i: (0, 0)),
        ],
        out_specs=pl.BlockSpec((tq, DIFF_DV), lambda h, i: (i, h)),
        out_shape=jax.ShapeDtypeStruct((rows, GROUP_WIDTH), F32),
        scratch_shapes=[
            pltpu.VMEM((1, 2 * tq), F32),
            pltpu.VMEM((DV_EXT, 2 * tq), F32),
            pltpu.VMEM((2, ATTN_TK, 2 * tq), F32),
            pltpu.VMEM((2, ATTN_TK, 2 * tq), BF16),
        ],
        compiler_params=pltpu.CompilerParams(
            dimension_semantics=("parallel", "arbitrary"), vmem_limit_bytes=VMEM_LIMIT),
        name="diff_attn_ctx" if context_queries else "diff_attn_lat",
    )(lam, q, k, vt, out_gain)


def rope_tables(n_lat, rows):
    pos = jnp.arange(n_lat, dtype=jnp.int32)
    row = (pos // GRID_W).astype(F32)
    col = (pos % GRID_W).astype(F32)
    inv_freq = ROPE_BASE ** (-jnp.arange(ROPE_PAIRS, dtype=F32) / ROPE_PAIRS)
    ang_r, ang_c = row[:, None] * inv_freq, col[:, None] * inv_freq
    cos = jnp.concatenate([jnp.cos(ang_r), jnp.cos(ang_r), jnp.cos(ang_c), jnp.cos(ang_c)], axis=-1)
    sin = jnp.concatenate([-jnp.sin(ang_r), jnp.sin(ang_r), -jnp.sin(ang_c), jnp.sin(ang_c)], axis=-1)
    cos, sin = jnp.tile(cos, (1, 2)), jnp.tile(sin, (1, 2))
    pad = ((0, rows - n_lat), (0, 0))
    return jnp.pad(cos, pad, constant_values=1.0), jnp.pad(sin, pad)


def diff_lambda_scalar(lam_params, lam_init):
    lq1, lk1, lq2, lk2 = lam_params
    return (jnp.exp(jnp.sum(lq1 * lk1)) - jnp.exp(jnp.sum(lq2 * lk2)) + lam_init).reshape(1, 1)


GDN_QKV_WIDTH = GDN_HEADS * (2 * GDN_DK + GDN_DV)
GDN_PREP_TM = 512
HALO = 8


def _softplus(z):
    return jnp.maximum(z, 0.0) + jnp.log(1.0 + jnp.exp(-jnp.abs(z)))


def _conv3(x, prev_row, next_row, w, row0, n_lat, n_tok):
    tm = x.shape[0]
    r = lax.broadcasted_iota(jnp.int32, (tm, 1), 0)
    rows = row0 + r
    xm1 = jnp.where(r == 0, prev_row, pltpu.roll(x, 1, 0))
    xm1 = jnp.where((rows == 0) | (rows == n_lat), 0.0, xm1)
    xp1 = jnp.where(r == tm - 1, next_row, pltpu.roll(x, tm - 1, 0))
    xp1 = jnp.where((rows == n_lat - 1) | (rows == n_tok - 1), 0.0, xp1)
    return xm1 * w[0:1] + x * w[1:2] + xp1 * w[2:3]


def _gdn_prep_body(x_ref, xp_ref, xn_ref, ps_ref, cw_ref, a_ref, dt_ref, q_ref, k_ref, v_ref, gs_ref, *, n_lat, n_tok):
    tm = x_ref.shape[0]
    y = _conv3(x_ref[...], xp_ref[HALO - 1:HALO, :], xn_ref[0:1, :], cw_ref[...], pl.program_id(0) * tm, n_lat, n_tok)
    y = y * jax.nn.sigmoid(y)
    hk = GDN_HEADS * GDN_DK
    q, k = y[:, :hk], y[:, hk:2 * hk]
    q_ref[...] = q * lax.rsqrt(_group_mean_sq(q, GDN_DK) * GDN_DK + NORM_EPS) * (GDN_DK ** -0.5)
    k_ref[...] = k * lax.rsqrt(_group_mean_sq(k, GDN_DK) * GDN_DK + NORM_EPS)
    v_ref[...] = y[:, 2 * hk:]
    ps = ps_ref[...]
    lane = lax.broadcasted_iota(jnp.int32, ps.shape, 1)
    gs_ref[...] = jnp.where(lane < 2 * GDN_HEADS, jax.nn.sigmoid(ps), a_ref[...] * _softplus(ps + dt_ref[...]))


def gdn_prep(p_main, p_small, conv_w, neg_a, dt, n_lat, n_tok):
    rows = p_main.shape[0]
    tm = GDN_PREP_TM
    nb = rows // tm
    per = tm // HALO
    out = jax.ShapeDtypeStruct((rows, GROUP_WIDTH), F32)
    return pl.pallas_call(
        functools.partial(_gdn_prep_body, n_lat=n_lat, n_tok=n_tok),
        grid=(nb,),
        in_specs=[
            pl.BlockSpec((tm, GDN_QKV_WIDTH), lambda i: (i, 0)),
            pl.BlockSpec((HALO, GDN_QKV_WIDTH), lambda i: (jnp.maximum(i * per - 1, 0), 0)),
            pl.BlockSpec((HALO, GDN_QKV_WIDTH), lambda i: (jnp.minimum((i + 1) * per, nb * per - 1), 0)),
            pl.BlockSpec((tm, SMALL_WIDTH), lambda i: (i, 0)),
            pl.BlockSpec((3, GDN_QKV_WIDTH), lambda i: (0, 0)),
            pl.BlockSpec((1, SMALL_WIDTH), lambda i: (0, 0)),
            pl.BlockSpec((1, SMALL_WIDTH), lambda i: (0, 0)),
        ],
        out_specs=[pl.BlockSpec((tm, GROUP_WIDTH), lambda i: (i, 0))] * 3 + [pl.BlockSpec((tm, SMALL_WIDTH), lambda i: (i, 0))],
        out_shape=[out, out, out, jax.ShapeDtypeStruct((rows, SMALL_WIDTH), F32)],
        compiler_params=pltpu.CompilerParams(dimension_semantics=("parallel",), vmem_limit_bytes=VMEM_LIMIT),
        name="gdn_prep",
    )(p_main, p_main, p_main, p_small, conv_w, neg_a, dt)


PAIR = 2 * SCAN_CHUNK


def _stack_pair(a):
    first = lax.broadcasted_iota(jnp.int32, a.shape, 1) < (PAIR // 2)
    return jnp.concatenate([jnp.where(first, a, 0.0), jnp.where(first, 0.0, a)], axis=0)


def _pair_col(arr, c0, c1):
    return jnp.concatenate([arr[:, c0:c0 + 1], arr[:, c1:c1 + 1]], axis=0)


def _dot_nt(a, b):
    return lax.dot_general(a, b, (((1,), (1,)), ((), ())), preferred_element_type=F32)


def _dot_tn(a, b):
    return lax.dot_general(a, b, (((0,), (0,)), ((), ())), preferred_element_type=F32)


def _chunk_masks(d):
    C = SCAN_CHUNK
    r = lax.broadcasted_iota(jnp.int32, (PAIR, PAIR), 0)
    c = lax.broadcasted_iota(jnp.int32, (PAIR, PAIR), 1)
    same = (r // C) == (c // C)
    ri, ci = r % C, c % C
    incl = same & ((ci <= ri) if d == 0 else (ci >= ri))
    strict = same & ((ci < ri) if d == 0 else (ci > ri))
    r1 = lax.broadcasted_iota(jnp.int32, (C, C), 0)
    c1 = lax.broadcasted_iota(jnp.int32, (C, C), 1)
    tri = jnp.where((c1 <= r1) if d == 0 else (c1 >= r1), 1.0, 0.0).astype(F32)
    return incl, strict, tri, r == c


def _gdn_scan_body(qf, kf, vf, gf, qb, kb, vb, gb, of_ref, ob_ref, s_ref):
    C = SCAN_CHUNK
    dot = functools.partial(jnp.dot, preferred_element_type=F32)

    @pl.when(pl.program_id(0) == 0)
    def _():
        s_ref[...] = jnp.zeros(s_ref.shape, F32)

    top = lax.broadcasted_iota(jnp.int32, (PAIR, 1), 0) < C
    units = []
    for d, (q_ref, k_ref, v_ref, g_ref, o_ref) in enumerate(((qf, kf, vf, gf, of_ref), (qb, kb, vb, gb, ob_ref))):
        incl, strict, tri, eye = _chunk_masks(d)
        gs = g_ref[...]
        g_all = jnp.dot(tri, gs, preferred_element_type=F32, precision=HI)
        g_end = g_all[C - 1:C] if d == 0 else g_all[0:1]
        g_t = g_all.T
        for p in range(GDN_HEADS // 2):
            sl = slice(PAIR * p, PAIR * (p + 1))
            b0 = GDN_HEADS * d + 2 * p
            c0 = 2 * GDN_HEADS + b0
            u = dict(d=d, p=p, sl=sl, o_ref=o_ref, incl=incl, strict=strict, eye=eye)
            u['K'], u['Q'], u['V'] = _stack_pair(k_ref[:, sl]), _stack_pair(q_ref[:, sl]), _stack_pair(v_ref[:, sl])
            u['g_col'] = _pair_col(g_all, c0, c0 + 1)
            u['beta'] = _pair_col(gs, b0, b0 + 1)
            u['g_row'] = jnp.concatenate([g_t[c0:c0 + 1, :], g_t[c0 + 1:c0 + 2, :]], axis=1)
            u['gl_col'] = jnp.where(top, g_end[:, c0:c0 + 1], g_end[:, c0 + 1:c0 + 2])
            units.append(u)
    for u in units:
        u['decay'] = jnp.exp(jnp.where(u['incl'], u['g_col'] - u['g_row'], NEG_BIG))
        u['eg'] = jnp.exp(u['g_col'])
    for u in units:
        u['A'] = jnp.where(u['strict'], -(u['beta'] * _dot_nt(u['K'], u['K']) * u['decay']), 0.0)
        u['P'] = jnp.where(u['eye'], 1.0, 0.0) + u['A']
        u['negL'] = u['A']
    for u in units:
        u['attn'] = _dot_nt(u['Q'], u['K']) * u['decay']
    for _ in range(5):
        for u in units:
            u['A'] = dot(u['A'], u['A'])
        for u in units:
            u['P'] = u['P'] + dot(u['P'], u['A'])
    for u in units:
        u['R'] = (jnp.where(u['eye'], 1.0, 0.0) - u['P']) + jnp.dot(u['negL'], u['P'], preferred_element_type=F32, precision=HI)
    for u in units:
        u['P'] = u['P'] + dot(u['P'], u['R'])
    for u in units:
        u['uw'] = dot(u['P'], jnp.concatenate([u['V'] * u['beta'], u['K'] * (u['beta'] * u['eg'])], axis=1))
    for u in units:
        u['S'] = s_ref[u['d'], u['p']]
        u['ws'] = dot(jnp.concatenate([u['uw'][:, PAIR:], u['Q'] * u['eg']], axis=0), u['S'])
    for u in units:
        u['v_new'] = u['uw'][:, :PAIR] - u['ws'][:PAIR]
        o_st = u['ws'][PAIR:] + dot(u['attn'], u['v_new'])
        u['o_ref'][:, u['sl']] = o_st[:C] + o_st[C:]
    for u in units:
        s_ref[u['d'], u['p']] = (u['S'] * jnp.exp(u['gl_col'])
                                 + _dot_tn(u['K'] * jnp.exp(u['gl_col'] - u['g_col']), u['v_new']))


def _scan_order(n_lat, n_ctx):
    nl, nc = n_lat // SCAN_CHUNK, n_ctx // SCAN_CHUNK
    fwd = lambda t: (jnp.where(t < nc, nl + t, t - nc), 0)
    bwd = lambda t: (nl + nc - 1 - t, 0)
    return nl + nc, fwd, bwd


def gdn_scan(q, k, v, gs, n_lat, n_ctx):
    steps, fwd, bwd = _scan_order(n_lat, n_ctx)
    wide = lambda f: pl.BlockSpec((SCAN_CHUNK, GROUP_WIDTH), f)
    small = lambda f: pl.BlockSpec((SCAN_CHUNK, SMALL_WIDTH), f)
    out = jax.ShapeDtypeStruct((n_lat + n_ctx, GROUP_WIDTH), F32)
    return pl.pallas_call(
        _gdn_scan_body,
        grid=(steps,),
        in_specs=[wide(fwd), wide(fwd), wide(fwd), small(fwd), wide(bwd), wide(bwd), wide(bwd), small(bwd)],
        out_specs=[wide(fwd), wide(bwd)],
        out_shape=[out, out],
        scratch_shapes=[pltpu.VMEM((2, GDN_HEADS // 2, PAIR, PAIR), F32)],
        compiler_params=pltpu.CompilerParams(dimension_semantics=("arbitrary",), vmem_limit_bytes=VMEM_LIMIT),
        name="gdn_scan",
    )(q, k, v, gs, q, k, v, gs)


def gdn_gate_vectors(a_log, dt_bias):
    n = 2 * GDN_HEADS
    pad = lambda t: jnp.pad(t.reshape(1, n), ((0, 0), (n, SMALL_WIDTH - 2 * n)))
    return pad(-jnp.exp(a_log)), pad(dt_bias)


GLA_QK_WIDTH = GLA_HEADS * GLA_DK


def _gla_scan_body(qf, kf, vf, pf, qb, kb, vb, pb, wup_ref, bias_ref, of_ref, ob_ref, s_ref):
    C = SCAN_CHUNK
    dot = functools.partial(jnp.dot, preferred_element_type=F32)

    @pl.when(pl.program_id(0) == 0)
    def _():
        s_ref[...] = jnp.zeros(s_ref.shape, F32)

    units = []
    for d, (q_ref, k_ref, v_ref, p_ref, o_ref) in enumerate(((qf, kf, vf, pf, of_ref), (qb, kb, vb, pb, ob_ref))):
        incl, _, tri, _ = _chunk_masks(d)
        z = dot(p_ref[...], wup_ref[...]) + bias_ref[...]
        z = z[:, GLA_QK_WIDTH * d:GLA_QK_WIDTH * (d + 1)]
        log_a = (jnp.minimum(z, 0.0) - jnp.log(1.0 + jnp.exp(-jnp.abs(z)))) * (1.0 / GLA_TAU)
        b = jnp.dot(tri, log_a, preferred_element_type=F32, precision=HI)
        b_end = b[C - 1:C] if d == 0 else b[0:1]
        q = q_ref[...] * (GLA_DK ** -0.5)
        k_out = k_ref[...] * jnp.exp(b_end - b)
        q_in = q * jnp.exp(b - b_end)
        q_st = q * jnp.exp(b)
        a_col = jnp.broadcast_to(jnp.exp(b_end), (C, GLA_QK_WIDTH)).T[:, 0:1]
        v = v_ref[...]
        for p in range(GLA_HEADS // 2):
            sl = slice(PAIR * p, PAIR * (p + 1))
            v0, v1 = v[:, 2 * p * GLA_DV:(2 * p + 1) * GLA_DV], v[:, (2 * p + 1) * GLA_DV:(2 * p + 2) * GLA_DV]
            units.append(dict(d=d, p=p, o_ref=o_ref, incl=incl, V=jnp.concatenate([v0, v1], axis=0),
                              K=_stack_pair(k_out[:, sl]), Qin=_stack_pair(q_in[:, sl]), Qst=_stack_pair(q_st[:, sl]),
                              a=a_col[sl]))
    for u in units:
        u['attn'] = jnp.where(u['incl'], _dot_nt(u['Qin'], u['K']), 0.0)
        u['S'] = s_ref[u['d'], u['p']]
    for u in units:
        u['o'] = dot(u['Qst'], u['S'])
        u['dS'] = _dot_tn(u['K'], u['V'])
    for u in units:
        o_st = u['o'] + dot(u['attn'], u['V'])
        p = u['p']
        u['o_ref'][:, 2 * p * GLA_DV:(2 * p + 1) * GLA_DV] = o_st[:C]
        u['o_ref'][:, (2 * p + 1) * GLA_DV:(2 * p + 2) * GLA_DV] = o_st[C:]
        s_ref[u['d'], p] = u['a'] * u['S'] + u['dS']


def gla_scan(p_main, p_small, w_up, bias, n_lat, n_ctx):
    steps, fwd, bwd = _scan_order(n_lat, n_ctx)
    qc, kc, vc = (MAIN_OFFSETS[n][0] for n in ('gla_q', 'gla_k', 'gla_v'))
    at = lambda f, blk: (lambda t: (f(t)[0], blk))
    qk = lambda f, off: pl.BlockSpec((SCAN_CHUNK, GLA_QK_WIDTH), at(f, off // GLA_QK_WIDTH))
    vv = lambda f: pl.BlockSpec((SCAN_CHUNK, GROUP_WIDTH), at(f, vc // GROUP_WIDTH))
    small = lambda f: pl.BlockSpec((SCAN_CHUNK, SMALL_WIDTH), f)
    const = lambda shape: pl.BlockSpec(shape, lambda t: (0, 0))
    out = jax.ShapeDtypeStruct((n_lat + n_ctx, GROUP_WIDTH), F32)
    return pl.pallas_call(
        _gla_scan_body,
        grid=(steps,),
        in_specs=[qk(fwd, qc), qk(fwd, kc), vv(fwd), small(fwd), qk(bwd, qc), qk(bwd, kc), vv(bwd), small(bwd),
                  const((SMALL_WIDTH, 2 * GLA_QK_WIDTH)), const((1, 2 * GLA_QK_WIDTH))],
        out_specs=[pl.BlockSpec((SCAN_CHUNK, GROUP_WIDTH), fwd), pl.BlockSpec((SCAN_CHUNK, GROUP_WIDTH), bwd)],
        out_shape=[out, out],
        scratch_shapes=[pltpu.VMEM((2, GLA_HEADS // 2, PAIR, GLA_DV), F32)],
        compiler_params=pltpu.CompilerParams(dimension_semantics=("arbitrary",), vmem_limit_bytes=VMEM_LIMIT),
        name="gla_scan",
    )(p_main, p_main, p_main, p_small, p_main, p_main, p_main, p_small, w_up, bias)


def gla_gate_weights(gate_up, gate_bias):
    off = SMALL_OFFSETS['gla_lowrank'][0]
    w = jnp.zeros((SMALL_WIDTH, 2 * GLA_QK_WIDTH), F32)
    for d in range(2):
        w = w.at[off + d * GLA_GATE_RANK:off + (d + 1) * GLA_GATE_RANK, d * GLA_QK_WIDTH:(d + 1) * GLA_QK_WIDTH].set(gate_up[d])
    return w, gate_bias.reshape(1, 2 * GLA_QK_WIDTH)


MIX_TM = 256
ROUTER_WIDTH = LANE


def _mixproj_body(x_ref, gf_ref, gb_ref, gate_ref, scb_ref, scc_ref, scx_ref, ccp_ref, ccn_ref, cxp_ref, cxn_ref,
                  lf_ref, lb_ref, r_ref, dl_ref, dc_ref, w_ref, g2_ref, gng_ref, gnl_ref, scw_ref, n2_ref,
                  shift_ref, scale_ref, wrh_ref, wrl_ref, rb_ref, xo_ref, h2_ref, rt_ref, *, n_lat, n_tok):
    tm = x_ref.shape[0]
    row0 = pl.program_id(0) * tm
    silu = lambda t: t * jax.nn.sigmoid(t)
    dot = functools.partial(jnp.dot, preferred_element_type=F32)

    og = gf_ref[...] + gb_ref[...]
    a = og * lax.rsqrt(_group_mean_sq(og, GDN_DV) + NORM_EPS) * gng_ref[...] * silu(gate_ref[...])
    u = scc_ref[...] * scx_ref[...]
    u_prev = ccp_ref[HALO - 1:HALO, :] * cxp_ref[HALO - 1:HALO, :]
    u_next = ccn_ref[0:1, :] * cxn_ref[0:1, :]
    b = scb_ref[...] * _conv3(u, u_prev, u_next, scw_ref[...], row0, n_lat, n_tok)
    ol = lf_ref[...] + lb_ref[...]
    g = ol * lax.rsqrt(_group_mean_sq(ol, GLA_DV) + NORM_EPS) * gnl_ref[...] * silu(r_ref[...])
    d = jnp.where(row0 < n_lat, dl_ref[...], dc_ref[...])

    W = GROUP_WIDTH
    y = (dot(a.astype(BF16), w_ref[0:W, :]) + dot(b.astype(BF16), w_ref[W:2 * W, :])
         + dot(g.astype(BF16), w_ref[2 * W:3 * W, :]) + dot(d.astype(BF16), w_ref[3 * W:4 * W, :]))
    xn = x_ref[...] + g2_ref[...] * y
    xo_ref[...] = xn
    h2 = xn * lax.rsqrt(jnp.mean(xn * xn, axis=-1, keepdims=True) + NORM_EPS) * n2_ref[...]
    h2 = h2 * (1.0 + scale_ref[...]) + shift_ref[...]
    hb = h2.astype(BF16)
    h2_ref[...] = hb
    h_lo = (h2 - hb.astype(F32)).astype(BF16)
    logits = dot(hb, wrh_ref[...]) + (dot(h_lo, wrh_ref[...]) + dot(hb, wrl_ref[...])) + rb_ref[...]
    rt_ref[...] = _route(logits)


def _route(lg):
    lane = lax.broadcasted_iota(jnp.int32, lg.shape, 1)
    big = 4 * LANE
    rmax = lambda t: jnp.max(t, axis=1, keepdims=True)
    rsum = lambda t: jnp.sum(t, axis=1, keepdims=True)
    first_lane = lambda cond: jnp.min(jnp.where(cond, lane, big), axis=1, keepdims=True)

    is_g = lane < N_GROUPS
    g_exp = jnp.where(is_g, jnp.exp(lg - rmax(jnp.where(is_g, lg, NEG_BIG))), 0.0)
    g_prob = g_exp / rsum(g_exp)
    g_top = rmax(g_prob)
    g_idx = first_lane(is_g & (g_prob == g_top))

    e_lane = lane - N_GROUPS
    in_g = (e_lane >= 0) & (e_lane < N_EXPERTS) & ((e_lane // EXPERTS_PER_GROUP) == g_idx)
    e_exp = jnp.where(in_g, jnp.exp(lg - rmax(jnp.where(in_g, lg, NEG_BIG))), 0.0)
    e_prob = e_exp / rsum(e_exp)
    p1 = rmax(jnp.where(in_g, e_prob, -1.0))
    i1 = first_lane(in_g & (e_prob == p1))
    rest = in_g & (lane != i1)
    p2 = rmax(jnp.where(rest, e_prob, -1.0))
    i2 = first_lane(rest & (e_prob == p2))
    w1 = g_top * (p1 / (p1 + p2))
    w2 = g_top * (p2 / (p1 + p2))
    ids = jnp.where(lane == 0, i1 - N_GROUPS, i2 - N_GROUPS).astype(F32)
    return jnp.where(lane < 2, ids, jnp.where(lane == 2, w1, jnp.where(lane == 3, w2, 0.0)))


def mixproj(xa, pm, gdn_f, gdn_b, gla_f, gla_b, d_lat, d_ctx, w_out, gate, gdn_gain, gla_gain, sc_w, norm2,
            shift, scale, w_router, b_router, n_lat, n_ctx):
    w_hi = w_router.astype(BF16)
    w_lo = (w_router - w_hi.astype(F32)).astype(BF16)
    rows = xa.shape[0]
    n_tok = n_lat + n_ctx
    tm = MIX_TM
    assert n_lat % tm == 0 and n_ctx == tm
    lat_tiles = n_lat // tm
    per = tm // HALO
    last_halo = rows // HALO - 1
    seg = lambda name: pl.BlockSpec((tm, GROUP_WIDTH), lambda i, c=MAIN_OFFSETS[name][0] // GROUP_WIDTH: (i, c))
    halo_prev = lambda name: pl.BlockSpec(
        (HALO, GROUP_WIDTH), lambda i, c=MAIN_OFFSETS[name][0] // GROUP_WIDTH: (jnp.maximum(i * per - 1, 0), c))
    halo_next = lambda name: pl.BlockSpec(
        (HALO, GROUP_WIDTH), lambda i, c=MAIN_OFFSETS[name][0] // GROUP_WIDTH: (jnp.minimum((i + 1) * per, last_halo), c))
    tok = pl.BlockSpec((tm, GROUP_WIDTH), lambda i: (i, 0))
    full = pl.BlockSpec((tm, D_MODEL), lambda i: (i, 0))
    mod = pl.BlockSpec((None, 1, D_MODEL), lambda i: (i // lat_tiles, 0, 0))
    const = lambda shape: pl.BlockSpec(shape, lambda i: (0,) * len(shape))
    return pl.pallas_call(
        functools.partial(_mixproj_body, n_lat=n_lat, n_tok=n_tok),
        grid=(n_tok // tm,),
        in_specs=[
            full, tok, tok, seg('gdn_gate'), seg('sc_b'), seg('sc_c'), seg('sc_x'),
            halo_prev('sc_c'), halo_next('sc_c'), halo_prev('sc_x'), halo_next('sc_x'),
            tok, tok, seg('gla_r'),
            pl.BlockSpec((tm, GROUP_WIDTH), lambda i: (jnp.minimum(i, lat_tiles - 1), 0)),
            pl.BlockSpec((tm, GROUP_WIDTH), lambda i: (0, 0)),
            const((D_MODEL, D_MODEL)), mod, const((1, GROUP_WIDTH)), const((1, GROUP_WIDTH)), const((3, GROUP_WIDTH)),
            const((1, D_MODEL)), mod, mod, const((D_MODEL, ROUTER_WIDTH)), const((D_MODEL, ROUTER_WIDTH)),
            const((1, ROUTER_WIDTH)),
        ],
        out_specs=[full, full, pl.BlockSpec((tm, ROUTER_WIDTH), lambda i: (i, 0))],
        out_shape=[
            jax.ShapeDtypeStruct((rows, D_MODEL), F32),
            jax.ShapeDtypeStruct((n_tok, D_MODEL), BF16),
            jax.ShapeDtypeStruct((n_tok, ROUTER_WIDTH), F32),
        ],
        input_output_aliases={0: 0},
        compiler_params=pltpu.CompilerParams(dimension_semantics=("parallel",), vmem_limit_bytes=VMEM_LIMIT),
        name="mixproj",
    )(xa, gdn_f, gdn_b, pm, pm, pm, pm, pm, pm, pm, pm, gla_f, gla_b, pm, d_lat, d_ctx, w_out, gate,
      gdn_gain, gla_gain, sc_w, norm2, shift, scale, w_hi, w_lo, b_router)


def _ffn_body(be_ref, nu_ref, x_ref, w1_ref, w3_ref, w2_ref, o_ref, w1b_ref, w3b_ref, w2b_ref):
    b = pl.program_id(0)
    used = b < nu_ref[0]
    new_expert = jnp.logical_or(b == 0, be_ref[b] != be_ref[jnp.maximum(b - 1, 0)])

    @pl.when(jnp.logical_and(used, new_expert))
    def _():
        w1b_ref[...] = w1_ref[...].astype(BF16)
        w3b_ref[...] = w3_ref[...].astype(BF16)
        w2b_ref[...] = w2_ref[...].astype(BF16)

    @pl.when(used)
    def _():
        x = x_ref[...]
        a = jnp.dot(x, w1b_ref[...], preferred_element_type=F32)
        g = jnp.dot(x, w3b_ref[...], preferred_element_type=F32)
        mid = (a * jax.nn.sigmoid(a) * g).astype(BF16)
        o_ref[...] = jnp.dot(mid, w2b_ref[...], preferred_element_type=F32)

    @pl.when(jnp.logical_not(used))
    def _():
        o_ref[...] = jnp.zeros(o_ref.shape, F32)


def expert_ffn(block_expert, n_used, xs, w1, w3, w2, layer):
    L = xs.shape[0]
    n_blocks = L // EXPERT_BLOCK
    wspec = lambda shape: pl.BlockSpec((None, None) + shape, lambda b, be, nu: (layer, be[b], 0, 0))
    return pl.pallas_call(
        _ffn_body,
        grid_spec=pltpu.PrefetchScalarGridSpec(
            num_scalar_prefetch=2,
            grid=(n_blocks,),
            in_specs=[
                pl.BlockSpec((EXPERT_BLOCK, D_MODEL), lambda b, be, nu: (b, 0)),
                wspec((D_MODEL, D_EXPERT)), wspec((D_MODEL, D_EXPERT)), wspec((D_EXPERT, D_MODEL)),
            ],
            out_specs=pl.BlockSpec((EXPERT_BLOCK, D_MODEL), lambda b, be, nu: (b, 0)),
            scratch_shapes=[
                pltpu.VMEM((D_MODEL, D_EXPERT), BF16),
                pltpu.VMEM((D_MODEL, D_EXPERT), BF16),
                pltpu.VMEM((D_EXPERT, D_MODEL), BF16),
            ],
        ),
        out_shape=jax.ShapeDtypeStruct((L, D_MODEL), F32),
        compiler_params=pltpu.CompilerParams(dimension_semantics=("arbitrary",), vmem_limit_bytes=VMEM_LIMIT),
        name="expert_ffn",
    )(block_expert, n_used, xs, w1, w3, w2)


def hier_moe_pallas(hb, routing, w1, w3, w2, layer):
    N, D = hb.shape
    expert_idx = routing[:, 0:TOP_K].astype(jnp.int32)
    weights = routing[:, TOP_K:2 * TOP_K]

    E, M, K = N_EXPERTS, EXPERT_BLOCK, TOP_K
    A = N * K
    flat_e = expert_idx.reshape(-1).astype(jnp.int32)
    order = jnp.argsort(flat_e)
    e_s = flat_e[order]
    t_s = (order // K).astype(jnp.int32)
    counts = jnp.bincount(flat_e, length=E).astype(jnp.int32)
    padded = (counts + M - 1) // M * M
    pad_end = jnp.cumsum(padded)
    pad_start = pad_end - padded
    start = jnp.cumsum(counts) - counts
    dest = pad_start[e_s] + (jnp.arange(A, dtype=jnp.int32) - start[e_s])
    n_blocks = -(-A // M) + E
    L = n_blocks * M
    slot_tok = jnp.full((L,), N, jnp.int32).at[dest].set(t_s)
    block_expert = jnp.minimum(
        jnp.searchsorted(pad_end, jnp.arange(n_blocks, dtype=jnp.int32) * M, side='right'), E - 1).astype(jnp.int32)
    n_used = (pad_end[-1] // M).astype(jnp.int32).reshape(1)
    h_pad = jnp.concatenate([hb, jnp.zeros((1, D), BF16)], axis=0)
    xs = h_pad[slot_tok]
    y = expert_ffn(block_expert, n_used, xs, w1, w3, w2, layer)
    pos = jnp.zeros((A,), jnp.int32).at[order].set(dest).reshape(N, K)
    return y[pos[:, 0]] * weights[:, 0:1] + y[pos[:, 1]] * weights[:, 1:2]


def _prep_w_in(w):
    cols = lambda names: jnp.concatenate(
        [w[:, PROJ_OFFSETS[n][0]:PROJ_OFFSETS[n][0] + PROJ_OFFSETS[n][1]] for n in names], axis=1)
    small = cols(SMALL_GROUPS)
    small = jnp.pad(small, ((0, 0), (0, SMALL_WIDTH - small.shape[1])))
    return cols(MAIN_GROUPS).astype(BF16), small.astype(BF16)


def kernel(x, c, ctx, c_ctx, w_mod, b_mod, norm1, norm2, w_in, w_out, gdn_conv, gdn_a_log, gdn_dt_bias, gdn_out_norm, sc_conv, gla_gate_up, gla_gate_bias, gla_out_norm, diff_q_norm, diff_k_norm, diff_lambda, diff_out_norm, router_group, router_group_bias, router_expert, router_expert_bias, expert_w1, expert_w3, expert_w2):
    n_lat, n_ctx = x.shape[1], ctx.shape[1]
    n_tok = n_lat + n_ctx
    rows = -(-n_tok // ROW_TILE) * ROW_TILE
    pad_rows = lambda t: jnp.pad(t, ((0, rows - t.shape[0]), (0, 0)))
    cos, sin = rope_tables(n_lat, rows)
    is_lat = jnp.arange(rows, dtype=jnp.int32)[:, None] < n_lat

    xa = pad_rows(jnp.concatenate([x[0], ctx[0]], axis=0))
    cond, cond_ctx = jax.nn.silu(c), jax.nn.silu(c_ctx)
    for l in range(DEPTH):
        with_ctx_out = l < DEPTH - 1
        lam_init = 0.8 - 0.6 * math.exp(-0.3 * l)
        m = jnp.split(cond @ w_mod[l] + b_mod[l], 6, axis=-1)
        mc = jnp.split((cond_ctx @ w_mod[l] + b_mod[l])[None, :], 6, axis=-1)
        mod = [jnp.stack([m[i], mc[i]], axis=0) for i in range(6)]

        w_main, w_small = _prep_w_in(w_in[l])
        pm, ps = inproj(xa, norm1[l][None, :], mod[0], mod[1], w_main, w_small, n_lat)

        neg_a, dt = gdn_gate_vectors(gdn_a_log[l], gdn_dt_bias[l])
        gq, gk, gv, gs = gdn_prep(pm, ps, gdn_conv[l], neg_a, dt, n_lat, n_tok)
        gdn_f, gdn_b = gdn_scan(gq, gk, gv, gs, n_lat, n_ctx)

        w_up, up_bias = gla_gate_weights(gla_gate_up[l], gla_gate_bias[l])
        gla_f, gla_b = gla_scan(pm, ps, w_up, up_bias, n_lat, n_ctx)

        lam = diff_lambda_scalar(diff_lambda[l], lam_init)
        q_gain = jnp.tile(diff_q_norm[l], 2 * DIFF_HEADS)[None, :]
        k_gain = jnp.tile(diff_k_norm[l], 2 * DIFF_HEADS)[None, :]
        d_gain = diff_out_norm[l][:, None]
        dq, dk, dvt = attn_prep(pm, cos, sin, q_gain, k_gain)
        d_lat = diff_attention(lam, dq, dk, dvt, d_gain, 1.0 - lam_init, n_lat, n_ctx, False)
        if with_ctx_out:
            d_ctx = diff_attention(lam, dq, dk, dvt, d_gain, 1.0 - lam_init, n_lat, n_ctx, True)
        else:
            d_ctx = jnp.zeros((n_ctx, GROUP_WIDTH), F32)

        w_router = jnp.pad(jnp.concatenate([router_group[l], router_expert[l]], axis=1),
                           ((0, 0), (0, ROUTER_WIDTH - N_GROUPS - N_EXPERTS)))
        pad_lanes = lambda t: jnp.pad(t, ((0, 0), (0, ROUTER_WIDTH - t.shape[1])))
        b_router = pad_lanes(jnp.concatenate([router_group_bias[l], router_expert_bias[l]])[None, :])
        xa, h2, routing = mixproj(
            xa, pm, gdn_f, gdn_b, gla_f, gla_b, d_lat, d_ctx, w_out[l].astype(BF16), mod[2],
            jnp.tile(gdn_out_norm[l], GDN_HEADS)[None, :], jnp.tile(gla_out_norm[l], GLA_HEADS)[None, :],
            sc_conv[l], norm2[l][None, :], mod[3], mod[4], w_router, b_router, n_lat, n_ctx)

        n_moe = n_tok if with_ctx_out else n_lat
        y = hier_moe_pallas(h2[:n_moe], routing[:n_moe], expert_w1, expert_w3, expert_w2, l)
        xa = xa + jnp.where(is_lat, m[5], mc[5]) * pad_rows(y)
    return xa[:n_lat][None]
```

```python
import functools
import math

import jax
import jax.numpy as jnp
from jax import lax
from jax.experimental import pallas as pl
from jax.experimental.pallas import tpu as pltpu

D_MODEL = 2048
DEPTH = 2
GRID_W = 64
GROUP_WIDTH = 512
GDN_DK = 64
GDN_DV = 64
GDN_HEADS = GROUP_WIDTH // GDN_DV
SCAN_CHUNK = 64
SC_WIDTH = GROUP_WIDTH
GLA_DK = 64
GLA_DV = 128
GLA_HEADS = GROUP_WIDTH // GLA_DV
GLA_GATE_RANK = 16
GLA_TAU = 16.0
DIFF_DK = 64
DIFF_DV = 128
DIFF_HEADS = GROUP_WIDTH // DIFF_DV
ROPE_BASE = 10000.0
ROPE_PAIRS = DIFF_DK // 4
N_GROUPS = 4
EXPERTS_PER_GROUP = 8
N_EXPERTS = N_GROUPS * EXPERTS_PER_GROUP
TOP_K = 2
D_EXPERT = D_MODEL // 4
EXPERT_BLOCK = 256
NORM_EPS = 1e-6
F32 = jnp.float32
BF16 = jnp.bfloat16
HI = lax.Precision.HIGHEST

PROJ_LAYOUT = (
    ('gdn_qkv', GDN_HEADS * (2 * GDN_DK + GDN_DV)),
    ('gdn_gate', GDN_HEADS * GDN_DV),
    ('gdn_beta', 2 * GDN_HEADS),
    ('gdn_alpha', 2 * GDN_HEADS),
    ('sc_b', SC_WIDTH),
    ('sc_c', SC_WIDTH),
    ('sc_x', SC_WIDTH),
    ('gla_q', GLA_HEADS * GLA_DK),
    ('gla_k', GLA_HEADS * GLA_DK),
    ('gla_v', GLA_HEADS * GLA_DV),
    ('gla_r', GLA_HEADS * GLA_DV),
    ('gla_lowrank', 2 * GLA_GATE_RANK),
    ('diff_q', DIFF_HEADS * 2 * DIFF_DK),
    ('diff_k', DIFF_HEADS * 2 * DIFF_DK),
    ('diff_v', DIFF_HEADS * DIFF_DV),
)
PROJ_OFFSETS = {}
_off = 0
for _name, _size in PROJ_LAYOUT:
    PROJ_OFFSETS[_name] = (_off, _size)
    _off += _size

MAIN_GROUPS = ('gdn_qkv', 'gdn_gate', 'sc_b', 'sc_c', 'sc_x', 'gla_q', 'gla_k', 'gla_v', 'gla_r',
               'diff_q', 'diff_k', 'diff_v')
SMALL_GROUPS = ('gdn_beta', 'gdn_alpha', 'gla_lowrank')
MAIN_OFFSETS = {}
_off = 0
for _name in MAIN_GROUPS:
    MAIN_OFFSETS[_name] = (_off, PROJ_OFFSETS[_name][1])
    _off += PROJ_OFFSETS[_name][1]
MAIN_WIDTH = _off
SMALL_OFFSETS = {}
_off = 0
for _name in SMALL_GROUPS:
    SMALL_OFFSETS[_name] = (_off, PROJ_OFFSETS[_name][1])
    _off += PROJ_OFFSETS[_name][1]
LANE = 128
SMALL_WIDTH = LANE

VMEM_LIMIT = 48 * 1024 * 1024
ROW_TILE = 1024


INPROJ_TM = 1024
INPROJ_TN = 512


def _inproj_body(x_ref, gain_ref, shift_ref, scale_ref, w_ref, ws_ref, o_ref, os_ref, h_ref):
    @pl.when(pl.program_id(1) == 0)
    def _():
        x = x_ref[...]
        y = x * lax.rsqrt(jnp.mean(x * x, axis=-1, keepdims=True) + NORM_EPS) * gain_ref[...]
        hb = (y * (1.0 + scale_ref[...]) + shift_ref[...]).astype(BF16)
        h_ref[...] = hb
        os_ref[...] = jnp.dot(hb, ws_ref[...], preferred_element_type=F32)

    o_ref[...] = jnp.dot(h_ref[...], w_ref[...], preferred_element_type=F32)


def inproj(x, gain, shift, scale, w_main, w_small, n_lat):
    rows = x.shape[0]
    tm = INPROJ_TM
    lat_tiles = n_lat // tm
    mod = pl.BlockSpec((None, 1, D_MODEL), lambda i, j: (i // lat_tiles, 0, 0))
    return pl.pallas_call(
        _inproj_body,
        grid=(rows // tm, MAIN_WIDTH // INPROJ_TN),
        in_specs=[
            pl.BlockSpec((tm, D_MODEL), lambda i, j: (i, 0)),
            pl.BlockSpec((1, D_MODEL), lambda i, j: (0, 0)),
            mod, mod,
            pl.BlockSpec((D_MODEL, INPROJ_TN), lambda i, j: (0, j)),
            pl.BlockSpec((D_MODEL, SMALL_WIDTH), lambda i, j: (0, 0)),
        ],
        out_specs=[
            pl.BlockSpec((tm, INPROJ_TN), lambda i, j: (i, j)),
            pl.BlockSpec((tm, SMALL_WIDTH), lambda i, j: (i, 0)),
        ],
        out_shape=[
            jax.ShapeDtypeStruct((rows, MAIN_WIDTH), F32),
            jax.ShapeDtypeStruct((rows, SMALL_WIDTH), F32),
        ],
        scratch_shapes=[pltpu.VMEM((tm, D_MODEL), BF16)],
        compiler_params=pltpu.CompilerParams(
            dimension_semantics=("parallel", "arbitrary"), vmem_limit_bytes=VMEM_LIMIT),
        name="inproj",
    )(x, gain, shift, scale, w_main, w_small)


HEAD_LANES = 2 * DIFF_DK
ATTN_TK = 512
ATTN_TQ = 1024
BF16_SUBLANES = 16
DV_EXT = DIFF_DV + BF16_SUBLANES
LOG2E = math.log2(math.e)
NEG_BIG = -1e30


def _group_mean_sq(x, width):
    n = x.shape[-1]
    gi = lax.broadcasted_iota(jnp.int32, (n, n), 0) // width
    gj = lax.broadcasted_iota(jnp.int32, (n, n), 1) // width
    ones = jnp.where(gi == gj, 1.0, 0.0).astype(BF16)
    rest = x * x
    total = None
    for _ in range(3):
        term = rest.astype(BF16)
        part = jnp.dot(term, ones, preferred_element_type=F32)
        total = part if total is None else total + part
        rest = rest - term.astype(F32)
    return total * (1.0 / width)


def _rope_swap(x):
    n = x.shape[-1]
    lane = lax.broadcasted_iota(jnp.int32, x.shape, x.ndim - 1)
    return jnp.where((lane % (2 * ROPE_PAIRS)) < ROPE_PAIRS,
                     pltpu.roll(x, n - ROPE_PAIRS, 1), pltpu.roll(x, ROPE_PAIRS, 1))


def _attn_prep_body(q_ref, k_ref, v_ref, cos_ref, sin_ref, qg_ref, kg_ref, qo_ref, ko_ref, vt_ref):
    cos = jnp.concatenate([cos_ref[...]] * DIFF_HEADS, axis=-1)
    sin = jnp.concatenate([sin_ref[...]] * DIFF_HEADS, axis=-1)

    def norm_rope(x, gain):
        y = x * lax.rsqrt(_group_mean_sq(x, DIFF_DK) + NORM_EPS) * gain
        return y * cos + _rope_swap(y) * sin

    q = norm_rope(q_ref[...], qg_ref[...]) * (DIFF_DK ** -0.5 * LOG2E)
    lane = lax.broadcasted_iota(jnp.int32, q.shape, 1)
    first = (lane % HEAD_LANES) < DIFF_DK
    qo_ref[0] = jnp.where(first, q, 0.0).astype(BF16)
    qo_ref[1] = jnp.where(first, 0.0, q).astype(BF16)
    ko_ref[...] = norm_rope(k_ref[...], kg_ref[...]).astype(BF16)
    v = v_ref[...]
    ones_tile = jnp.where(lax.broadcasted_iota(jnp.int32, (BF16_SUBLANES, v.shape[0]), 0) == 0, 1.0, 0.0)
    for h in range(DIFF_HEADS):
        vt_ref[h, 0] = jnp.concatenate([v[:, h * DIFF_DV:(h + 1) * DIFF_DV].T, ones_tile], axis=0).astype(BF16)


def attn_prep(p_main, cos, sin, q_gain, k_gain):
    rows = p_main.shape[0]
    tm = ATTN_TK
    nb = rows // tm
    col = lambda name: MAIN_OFFSETS[name][0] // GROUP_WIDTH
    seg = lambda name: pl.BlockSpec((tm, GROUP_WIDTH), lambda i, c=col(name): (i, c))
    tab = pl.BlockSpec((tm, HEAD_LANES), lambda i: (i, 0))
    vec = pl.BlockSpec((1, GROUP_WIDTH), lambda i: (0, 0))
    return pl.pallas_call(
        _attn_prep_body,
        grid=(nb,),
        in_specs=[seg('diff_q'), seg('diff_k'), seg('diff_v'), tab, tab, vec, vec],
        out_specs=[
            pl.BlockSpec((2, tm, GROUP_WIDTH), lambda i: (0, i, 0)),
            pl.BlockSpec((tm, GROUP_WIDTH), lambda i: (i, 0)),
            pl.BlockSpec((DIFF_HEADS, 1, DV_EXT, tm), lambda i: (0, i, 0, 0)),
        ],
        out_shape=[
            jax.ShapeDtypeStruct((2, rows, GROUP_WIDTH), BF16),
            jax.ShapeDtypeStruct((rows, GROUP_WIDTH), BF16),
            jax.ShapeDtypeStruct((DIFF_HEADS, nb, DV_EXT, tm), BF16),
        ],
        compiler_params=pltpu.CompilerParams(dimension_semantics=("parallel",), vmem_limit_bytes=VMEM_LIMIT),
        name="attn_prep",
    )(p_main, p_main, p_main, cos, sin, q_gain, k_gain)


def _attn_body(lam_ref, q_ref, k_ref, vt_ref, gn_ref, o_ref, m_ref, acc_ref, st_ref, p_ref, *,
               n_lat_chunks, n_ctx, tq, post_scale):
    tk = ATTN_TK
    q2 = q_ref[...].reshape(2 * tq, HEAD_LANES)
    m_ref[...] = jnp.full(m_ref.shape, NEG_BIG, F32)
    acc_ref[...] = jnp.zeros(acc_ref.shape, F32)

    def scores(kc):
        return lax.dot_general(kc, q2, (((1,), (1,)), ((), ())), preferred_element_type=F32)

    def softmax_update(st, pv):
        m_prev = m_ref[...]
        m_new = jnp.maximum(m_prev, jnp.max(st, axis=0, keepdims=True))
        alpha = jnp.exp2(m_prev - m_new)
        acc_ref[...] = alpha * (acc_ref[...] + pv)
        m_ref[...] = m_new
        return jnp.exp2(st - m_new).astype(BF16)

    ctx0 = n_lat_chunks * tk
    p_ctx = softmax_update(scores(k_ref[ctx0:ctx0 + n_ctx, :]), 0.0)
    acc_ref[...] = acc_ref[...] + jnp.dot(vt_ref[n_lat_chunks][:, :n_ctx], p_ctx, preferred_element_type=F32)

    if n_lat_chunks:
        st_ref[0] = scores(k_ref[0:tk, :])
        p_ref[1] = jnp.zeros(p_ref.shape[1:], BF16)

        def half(j, cur, nxt):
            jn = jnp.minimum(j + 1, n_lat_chunks - 1)
            st_ref[nxt] = scores(k_ref[pl.ds(pl.multiple_of(jn * tk, tk), tk), :])
            pv = jnp.dot(vt_ref[jnp.maximum(j - 1, 0)], p_ref[nxt], preferred_element_type=F32)
            p_ref[cur] = softmax_update(st_ref[cur], pv)

        def body(i, carry):
            half(2 * i, 0, 1)
            half(2 * i + 1, 1, 0)
            return carry
        lax.fori_loop(0, n_lat_chunks // 2, body, 0)
        acc_ref[...] = acc_ref[...] + jnp.dot(vt_ref[n_lat_chunks - 1], p_ref[1], preferred_element_type=F32)

    acc = acc_ref[...]
    l = acc[DIFF_DV:DIFF_DV + 1]
    o = acc[:DIFF_DV, :tq] / l[:, :tq] - lam_ref[0, 0] * (acc[:DIFF_DV, tq:] / l[:, tq:])
    y = o * lax.rsqrt(jnp.mean(o * o, axis=0, keepdims=True) + NORM_EPS) * (gn_ref[...] * post_scale)
    o_ref[...] = y.T


def diff_attention(lam, q, k, vt, out_gain, post_scale, n_lat, n_ctx, context_queries):
    lat_chunks = n_lat // ATTN_TK
    if context_queries:
        rows, tq, q_off, n_lat_chunks = n_ctx, n_ctx, n_lat // n_ctx, 0
        k_spec = pl.BlockSpec((ATTN_TK, HEAD_LANES), lambda h, i: (lat_chunks, h))
        vt_spec = pl.BlockSpec((None, 1, DV_EXT, ATTN_TK), lambda h, i: (h, lat_chunks, 0, 0))
    else:
        assert lat_chunks % 2 == 0 and n_lat % ATTN_TQ == 0
        rows, tq, q_off, n_lat_chunks = n_lat, ATTN_TQ, 0, lat_chunks
        k_spec = pl.BlockSpec(((lat_chunks + 1) * ATTN_TK, HEAD_LANES), lambda h, i: (0, h))
        vt_spec = pl.BlockSpec((None, lat_chunks + 1, DV_EXT, ATTN_TK), lambda h, i: (h, 0, 0, 0))
    return pl.pallas_call(
        functools.partial(_attn_body, n_lat_chunks=n_lat_chunks, n_ctx=n_ctx, tq=tq, post_scale=post_scale),
        grid=(DIFF_HEADS, rows // tq),
        in_specs=[
            pl.BlockSpec(memory_space=pltpu.SMEM),
            pl.BlockSpec((2, tq, HEAD_LANES), lambda h, i: (0, q_off + i, h)),
            k_spec, vt_spec,
            pl.BlockSpec((DIFF_DV, 1), lambda h, i: (0, 0)),
        ],
        out_specs=pl.BlockSpec((tq, DIFF_DV), lambda h, i: (i, h)),
        out_shape=jax.ShapeDtypeStruct((rows, GROUP_WIDTH), F32),
        scratch_shapes=[
            pltpu.VMEM((1, 2 * tq), F32),
            pltpu.VMEM((DV_EXT, 2 * tq), F32),
            pltpu.VMEM((2, ATTN_TK, 2 * tq), F32),
            pltpu.VMEM((2, ATTN_TK, 2 * tq), BF16),
        ],
        compiler_params=pltpu.CompilerParams(
            dimension_semantics=("parallel", "arbitrary"), vmem_limit_bytes=VMEM_LIMIT),
        name="diff_attn_ctx" if context_queries else "diff_attn_lat",
    )(lam, q, k, vt, out_gain)


def rope_tables(n_lat, rows):
    pos = jnp.arange(n_lat, dtype=jnp.int32)
    row = (pos // GRID_W).astype(F32)
    col = (pos % GRID_W).astype(F32)
    inv_freq = ROPE_BASE ** (-jnp.arange(ROPE_PAIRS, dtype=F32) / ROPE_PAIRS)
    ang_r, ang_c = row[:, None] * inv_freq, col[:, None] * inv_freq
    cos = jnp.concatenate([jnp.cos(ang_r), jnp.cos(ang_r), jnp.cos(ang_c), jnp.cos(ang_c)], axis=-1)
    sin = jnp.concatenate([-jnp.sin(ang_r), jnp.sin(ang_r), -jnp.sin(ang_c), jnp.sin(ang_c)], axis=-1)
    cos, sin = jnp.tile(cos, (1, 2)), jnp.tile(sin, (1, 2))
    pad = ((0, rows - n_lat), (0, 0))
    return jnp.pad(cos, pad, constant_values=1.0), jnp.pad(sin, pad)


def diff_lambda_scalar(lam_params, lam_init):
    lq1, lk1, lq2, lk2 = lam_params
    return (jnp.exp(jnp.sum(lq1 * lk1)) - jnp.exp(jnp.sum(lq2 * lk2)) + lam_init).reshape(1, 1)


GDN_QKV_WIDTH = GDN_HEADS * (2 * GDN_DK + GDN_DV)
GDN_PREP_TM = 512
HALO = 8


def _softplus(z):
    return jnp.maximum(z, 0.0) + jnp.log(1.0 + jnp.exp(-jnp.abs(z)))


def _conv3(x, prev_row, next_row, w, row0, n_lat, n_tok):
    tm = x.shape[0]
    r = lax.broadcasted_iota(jnp.int32, (tm, 1), 0)
    rows = row0 + r
    xm1 = jnp.where(r == 0, prev_row, pltpu.roll(x, 1, 0))
    xm1 = jnp.where((rows == 0) | (rows == n_lat), 0.0, xm1)
    xp1 = jnp.where(r == tm - 1, next_row, pltpu.roll(x, tm - 1, 0))
    xp1 = jnp.where((rows == n_lat - 1) | (rows == n_tok - 1), 0.0, xp1)
    return xm1 * w[0:1] + x * w[1:2] + xp1 * w[2:3]


def _gdn_prep_body(x_ref, xp_ref, xn_ref, ps_ref, cw_ref, a_ref, dt_ref, q_ref, k_ref, v_ref, gs_ref, *, n_lat, n_tok):
    tm = x_ref.shape[0]
    y = _conv3(x_ref[...], xp_ref[HALO - 1:HALO, :], xn_ref[0:1, :], cw_ref[...], pl.program_id(0) * tm, n_lat, n_tok)
    y = y * jax.nn.sigmoid(y)
    hk = GDN_HEADS * GDN_DK
    q, k = y[:, :hk], y[:, hk:2 * hk]
    q_ref[...] = q * lax.rsqrt(_group_mean_sq(q, GDN_DK) * GDN_DK + NORM_EPS) * (GDN_DK ** -0.5)
    k_ref[...] = k * lax.rsqrt(_group_mean_sq(k, GDN_DK) * GDN_DK + NORM_EPS)
    v_ref[...] = y[:, 2 * hk:]
    ps = ps_ref[...]
    lane = lax.broadcasted_iota(jnp.int32, ps.shape, 1)
    gs_ref[...] = jnp.where(lane < 2 * GDN_HEADS, jax.nn.sigmoid(ps), a_ref[...] * _softplus(ps + dt_ref[...]))


def gdn_prep(p_main, p_small, conv_w, neg_a, dt, n_lat, n_tok):
    rows = p_main.shape[0]
    tm = GDN_PREP_TM
    nb = rows // tm
    per = tm // HALO
    out = jax.ShapeDtypeStruct((rows, GROUP_WIDTH), F32)
    return pl.pallas_call(
        functools.partial(_gdn_prep_body, n_lat=n_lat, n_tok=n_tok),
        grid=(nb,),
        in_specs=[
            pl.BlockSpec((tm, GDN_QKV_WIDTH), lambda i: (i, 0)),
            pl.BlockSpec((HALO, GDN_QKV_WIDTH), lambda i: (jnp.maximum(i * per - 1, 0), 0)),
            pl.BlockSpec((HALO, GDN_QKV_WIDTH), lambda i: (jnp.minimum((i + 1) * per, nb * per - 1), 0)),
            pl.BlockSpec((tm, SMALL_WIDTH), lambda i: (i, 0)),
            pl.BlockSpec((3, GDN_QKV_WIDTH), lambda i: (0, 0)),
            pl.BlockSpec((1, SMALL_WIDTH), lambda i: (0, 0)),
            pl.BlockSpec((1, SMALL_WIDTH), lambda i: (0, 0)),
        ],
        out_specs=[pl.BlockSpec((tm, GROUP_WIDTH), lambda i: (i, 0))] * 3 + [pl.BlockSpec((tm, SMALL_WIDTH), lambda i: (i, 0))],
        out_shape=[out, out, out, jax.ShapeDtypeStruct((rows, SMALL_WIDTH), F32)],
        compiler_params=pltpu.CompilerParams(dimension_semantics=("parallel",), vmem_limit_bytes=VMEM_LIMIT),
        name="gdn_prep",
    )(p_main, p_main, p_main, p_small, conv_w, neg_a, dt)


PAIR = 2 * SCAN_CHUNK


def _stack_pair(a):
    first = lax.broadcasted_iota(jnp.int32, a.shape, 1) < (PAIR // 2)
    return jnp.concatenate([jnp.where(first, a, 0.0), jnp.where(first, 0.0, a)], axis=0)


def _pair_col(arr, c0, c1):
    return jnp.concatenate([arr[:, c0:c0 + 1], arr[:, c1:c1 + 1]], axis=0)


def _dot_nt(a, b):
    return lax.dot_general(a, b, (((1,), (1,)), ((), ())), preferred_element_type=F32)


def _dot_tn(a, b):
    return lax.dot_general(a, b, (((0,), (0,)), ((), ())), preferred_element_type=F32)


def _chunk_masks(d):
    C = SCAN_CHUNK
    r = lax.broadcasted_iota(jnp.int32, (PAIR, PAIR), 0)
    c = lax.broadcasted_iota(jnp.int32, (PAIR, PAIR), 1)
    same = (r // C) == (c // C)
    ri, ci = r % C, c % C
    incl = same & ((ci <= ri) if d == 0 else (ci >= ri))
    strict = same & ((ci < ri) if d == 0 else (ci > ri))
    r1 = lax.broadcasted_iota(jnp.int32, (C, C), 0)
    c1 = lax.broadcasted_iota(jnp.int32, (C, C), 1)
    tri = jnp.where((c1 <= r1) if d == 0 else (c1 >= r1), 1.0, 0.0).astype(F32)
    return incl, strict, tri, r == c


def _gdn_scan_body(qf, kf, vf, gf, qb, kb, vb, gb, of_ref, ob_ref, s_ref):
    C = SCAN_CHUNK
    dot = functools.partial(jnp.dot, preferred_element_type=F32)

    @pl.when(pl.program_id(0) == 0)
    def _():
        s_ref[...] = jnp.zeros(s_ref.shape, F32)

    top = lax.broadcasted_iota(jnp.int32, (PAIR, 1), 0) < C
    units = []
    for d, (q_ref, k_ref, v_ref, g_ref, o_ref) in enumerate(((qf, kf, vf, gf, of_ref), (qb, kb, vb, gb, ob_ref))):
        incl, strict, tri, eye = _chunk_masks(d)
        gs = g_ref[...]
        g_all = jnp.dot(tri, gs, preferred_element_type=F32, precision=HI)
        g_end = g_all[C - 1:C] if d == 0 else g_all[0:1]
        g_t = g_all.T
        for p in range(GDN_HEADS // 2):
            sl = slice(PAIR * p, PAIR * (p + 1))
            b0 = GDN_HEADS * d + 2 * p
            c0 = 2 * GDN_HEADS + b0
            u = dict(d=d, p=p, sl=sl, o_ref=o_ref, incl=incl, strict=strict, eye=eye)
            u['K'], u['Q'], u['V'] = _stack_pair(k_ref[:, sl]), _stack_pair(q_ref[:, sl]), _stack_pair(v_ref[:, sl])
            u['g_col'] = _pair_col(g_all, c0, c0 + 1)
            u['beta'] = _pair_col(gs, b0, b0 + 1)
            u['g_row'] = jnp.concatenate([g_t[c0:c0 + 1, :], g_t[c0 + 1:c0 + 2, :]], axis=1)
            u['gl_col'] = jnp.where(top, g_end[:, c0:c0 + 1], g_end[:, c0 + 1:c0 + 2])
            units.append(u)
    for u in units:
        u['decay'] = jnp.exp(jnp.where(u['incl'], u['g_col'] - u['g_row'], NEG_BIG))
        u['eg'] = jnp.exp(u['g_col'])
    for u in units:
        u['A'] = jnp.where(u['strict'], -(u['beta'] * _dot_nt(u['K'], u['K']) * u['decay']), 0.0)
        u['P'] = jnp.where(u['eye'], 1.0, 0.0) + u['A']
        u['negL'] = u['A']
    for u in units:
        u['attn'] = _dot_nt(u['Q'], u['K']) * u['decay']
    for _ in range(5):
        for u in units:
            u['A'] = dot(u['A'], u['A'])
        for u in units:
            u['P'] = u['P'] + dot(u['P'], u['A'])
    split = lambda t: (t.astype(BF16), (t - t.astype(BF16).astype(F32)).astype(BF16))
    for u in units:
        (l_hi, l_lo), (p_hi, p_lo) = split(u['negL']), split(u['P'])
        u['R'] = (jnp.where(u['eye'], 1.0, 0.0) - u['P']) + (dot(l_hi, p_hi) + (dot(l_hi, p_lo) + dot(l_lo, p_hi)))
    for u in units:
        u['P'] = u['P'] + dot(u['P'], u['R'])
    for u in units:
        u['uw'] = dot(u['P'], jnp.concatenate([u['V'] * u['beta'], u['K'] * (u['beta'] * u['eg'])], axis=1))
    for u in units:
        u['S'] = s_ref[u['d'], u['p']]
        u['ws'] = dot(jnp.concatenate([u['uw'][:, PAIR:], u['Q'] * u['eg']], axis=0), u['S'])
    for u in units:
        u['v_new'] = u['uw'][:, :PAIR] - u['ws'][:PAIR]
        o_st = u['ws'][PAIR:] + dot(u['attn'], u['v_new'])
        u['o_ref'][:, u['sl']] = o_st[:C] + o_st[C:]
    for u in units:
        s_ref[u['d'], u['p']] = (u['S'] * jnp.exp(u['gl_col'])
                                 + _dot_tn(u['K'] * jnp.exp(u['gl_col'] - u['g_col']), u['v_new']))


def _scan_order(n_lat, n_ctx):
    nl, nc = n_lat // SCAN_CHUNK, n_ctx // SCAN_CHUNK
    fwd = lambda t: (jnp.where(t < nc, nl + t, t - nc), 0)
    bwd = lambda t: (nl + nc - 1 - t, 0)
    return nl + nc, fwd, bwd


def gdn_scan(q, k, v, gs, n_lat, n_ctx):
    steps, fwd, bwd = _scan_order(n_lat, n_ctx)
    wide = lambda f: pl.BlockSpec((SCAN_CHUNK, GROUP_WIDTH), f)
    small = lambda f: pl.BlockSpec((SCAN_CHUNK, SMALL_WIDTH), f)
    out = jax.ShapeDtypeStruct((n_lat + n_ctx, GROUP_WIDTH), F32)
    return pl.pallas_call(
        _gdn_scan_body,
        grid=(steps,),
        in_specs=[wide(fwd), wide(fwd), wide(fwd), small(fwd), wide(bwd), wide(bwd), wide(bwd), small(bwd)],
        out_specs=[wide(fwd), wide(bwd)],
        out_shape=[out, out],
        scratch_shapes=[pltpu.VMEM((2, GDN_HEADS // 2, PAIR, PAIR), F32)],
        compiler_params=pltpu.CompilerParams(dimension_semantics=("arbitrary",), vmem_limit_bytes=VMEM_LIMIT),
        name="gdn_scan",
    )(q, k, v, gs, q, k, v, gs)


def gdn_gate_vectors(a_log, dt_bias):
    n = 2 * GDN_HEADS
    pad = lambda t: jnp.pad(t.reshape(1, n), ((0, 0), (n, SMALL_WIDTH - 2 * n)))
    return pad(-jnp.exp(a_log)), pad(dt_bias)


GLA_QK_WIDTH = GLA_HEADS * GLA_DK


def _gla_scan_body(qf, kf, vf, pf, qb, kb, vb, pb, wup_ref, bias_ref, of_ref, ob_ref, s_ref):
    C = SCAN_CHUNK
    dot = functools.partial(jnp.dot, preferred_element_type=F32)

    @pl.when(pl.program_id(0) == 0)
    def _():
        s_ref[...] = jnp.zeros(s_ref.shape, F32)

    units = []
    for d, (q_ref, k_ref, v_ref, p_ref, o_ref) in enumerate(((qf, kf, vf, pf, of_ref), (qb, kb, vb, pb, ob_ref))):
        incl, _, tri, _ = _chunk_masks(d)
        z = dot(p_ref[...], wup_ref[...]) + bias_ref[...]
        z = z[:, GLA_QK_WIDTH * d:GLA_QK_WIDTH * (d + 1)]
        log_a = (jnp.minimum(z, 0.0) - jnp.log(1.0 + jnp.exp(-jnp.abs(z)))) * (1.0 / GLA_TAU)
        b = jnp.dot(tri, log_a, preferred_element_type=F32, precision=HI)
        b_end = b[C - 1:C] if d == 0 else b[0:1]
        q = q_ref[...] * (GLA_DK ** -0.5)
        k_out = k_ref[...] * jnp.exp(b_end - b)
        q_in = q * jnp.exp(b - b_end)
        q_st = q * jnp.exp(b)
        a_col = jnp.broadcast_to(jnp.exp(b_end), (C, GLA_QK_WIDTH)).T[:, 0:1]
        v = v_ref[...]
        for p in range(GLA_HEADS // 2):
            sl = slice(PAIR * p, PAIR * (p + 1))
            v0, v1 = v[:, 2 * p * GLA_DV:(2 * p + 1) * GLA_DV], v[:, (2 * p + 1) * GLA_DV:(2 * p + 2) * GLA_DV]
            units.append(dict(d=d, p=p, o_ref=o_ref, incl=incl, V=jnp.concatenate([v0, v1], axis=0),
                              K=_stack_pair(k_out[:, sl]), Qin=_stack_pair(q_in[:, sl]), Qst=_stack_pair(q_st[:, sl]),
                              a=a_col[sl]))
    for u in units:
        u['attn'] = jnp.where(u['incl'], _dot_nt(u['Qin'], u['K']), 0.0)
        u['S'] = s_ref[u['d'], u['p']]
    for u in units:
        u['o'] = dot(u['Qst'], u['S'])
        u['dS'] = _dot_tn(u['K'], u['V'])
    for u in units:
        o_st = u['o'] + dot(u['attn'], u['V'])
        p = u['p']
        u['o_ref'][:, 2 * p * GLA_DV:(2 * p + 1) * GLA_DV] = o_st[:C]
        u['o_ref'][:, (2 * p + 1) * GLA_DV:(2 * p + 2) * GLA_DV] = o_st[C:]
        s_ref[u['d'], p] = u['a'] * u['S'] + u['dS']


def gla_scan(p_main, p_small, w_up, bias, n_lat, n_ctx):
    steps, fwd, bwd = _scan_order(n_lat, n_ctx)
    qc, kc, vc = (MAIN_OFFSETS[n][0] for n in ('gla_q', 'gla_k', 'gla_v'))
    at = lambda f, blk: (lambda t: (f(t)[0], blk))
    qk = lambda f, off: pl.BlockSpec((SCAN_CHUNK, GLA_QK_WIDTH), at(f, off // GLA_QK_WIDTH))
    vv = lambda f: pl.BlockSpec((SCAN_CHUNK, GROUP_WIDTH), at(f, vc // GROUP_WIDTH))
    small = lambda f: pl.BlockSpec((SCAN_CHUNK, SMALL_WIDTH), f)
    const = lambda shape: pl.BlockSpec(shape, lambda t: (0, 0))
    out = jax.ShapeDtypeStruct((n_lat + n_ctx, GROUP_WIDTH), F32)
    return pl.pallas_call(
        _gla_scan_body,
        grid=(steps,),
        in_specs=[qk(fwd, qc), qk(fwd, kc), vv(fwd), small(fwd), qk(bwd, qc), qk(bwd, kc), vv(bwd), small(bwd),
                  const((SMALL_WIDTH, 2 * GLA_QK_WIDTH)), const((1, 2 * GLA_QK_WIDTH))],
        out_specs=[pl.BlockSpec((SCAN_CHUNK, GROUP_WIDTH), fwd), pl.BlockSpec((SCAN_CHUNK, GROUP_WIDTH), bwd)],
        out_shape=[out, out],
        scratch_shapes=[pltpu.VMEM((2, GLA_HEADS // 2, PAIR, GLA_DV), F32)],
        compiler_params=pltpu.CompilerParams(dimension_semantics=("arbitrary",), vmem_limit_bytes=VMEM_LIMIT),
        name="gla_scan",
    )(p_main, p_main, p_main, p_small, p_main, p_main, p_main, p_small, w_up, bias)


def gla_gate_weights(gate_up, gate_bias):
    off = SMALL_OFFSETS['gla_lowrank'][0]
    w = jnp.zeros((SMALL_WIDTH, 2 * GLA_QK_WIDTH), F32)
    for d in range(2):
        w = w.at[off + d * GLA_GATE_RANK:off + (d + 1) * GLA_GATE_RANK, d * GLA_QK_WIDTH:(d + 1) * GLA_QK_WIDTH].set(gate_up[d])
    return w, gate_bias.reshape(1, 2 * GLA_QK_WIDTH)


MIX_TM = 256
ROUTER_WIDTH = LANE


def _mixproj_body(x_ref, gf_ref, gb_ref, gate_ref, scb_ref, scc_ref, scx_ref, ccp_ref, ccn_ref, cxp_ref, cxn_ref,
                  lf_ref, lb_ref, r_ref, dl_ref, dc_ref, w_ref, g2_ref, gng_ref, gnl_ref, scw_ref, n2_ref,
                  shift_ref, scale_ref, wrh_ref, wrl_ref, rb_ref, xo_ref, h2_ref, rt_ref, co_ref, cnt_ref, *,
                  n_lat, n_tok):
    tm = x_ref.shape[0]
    row0 = pl.program_id(0) * tm
    silu = lambda t: t * jax.nn.sigmoid(t)
    dot = functools.partial(jnp.dot, preferred_element_type=F32)

    og = gf_ref[...] + gb_ref[...]
    a = og * lax.rsqrt(_group_mean_sq(og, GDN_DV) + NORM_EPS) * gng_ref[...] * silu(gate_ref[...])
    u = scc_ref[...] * scx_ref[...]
    u_prev = ccp_ref[HALO - 1:HALO, :] * cxp_ref[HALO - 1:HALO, :]
    u_next = ccn_ref[0:1, :] * cxn_ref[0:1, :]
    b = scb_ref[...] * _conv3(u, u_prev, u_next, scw_ref[...], row0, n_lat, n_tok)
    ol = lf_ref[...] + lb_ref[...]
    g = ol * lax.rsqrt(_group_mean_sq(ol, GLA_DV) + NORM_EPS) * gnl_ref[...] * silu(r_ref[...])
    d = jnp.where(row0 < n_lat, dl_ref[...], dc_ref[...])

    W = GROUP_WIDTH
    y = (dot(a.astype(BF16), w_ref[0:W, :]) + dot(b.astype(BF16), w_ref[W:2 * W, :])
         + dot(g.astype(BF16), w_ref[2 * W:3 * W, :]) + dot(d.astype(BF16), w_ref[3 * W:4 * W, :]))
    xn = x_ref[...] + g2_ref[...] * y
    xo_ref[...] = xn
    h2 = xn * lax.rsqrt(jnp.mean(xn * xn, axis=-1, keepdims=True) + NORM_EPS) * n2_ref[...]
    h2 = h2 * (1.0 + scale_ref[...]) + shift_ref[...]
    hb = h2.astype(BF16)
    h2_ref[...] = hb
    h_lo = (h2 - hb.astype(F32)).astype(BF16)
    logits = dot(hb, wrh_ref[...]) + (dot(h_lo, wrh_ref[...]) + dot(hb, wrl_ref[...])) + rb_ref[...]
    @pl.when(pl.program_id(0) == 0)
    def _():
        cnt_ref[...] = jnp.zeros(cnt_ref.shape, F32)

    routing, counts = _route(logits, cnt_ref[...])
    rt_ref[...] = routing
    cnt_ref[...] = counts
    co_ref[...] = jnp.broadcast_to(counts, co_ref.shape)


def _route(lg, counts):
    lane = lax.broadcasted_iota(jnp.int32, lg.shape, 1)
    big = 4 * LANE
    rmax = lambda t: jnp.max(t, axis=1, keepdims=True)
    rsum = lambda t: jnp.sum(t, axis=1, keepdims=True)
    first_lane = lambda cond: jnp.min(jnp.where(cond, lane, big), axis=1, keepdims=True)

    is_g = lane < N_GROUPS
    g_exp = jnp.where(is_g, jnp.exp(lg - rmax(jnp.where(is_g, lg, NEG_BIG))), 0.0)
    g_prob = g_exp / rsum(g_exp)
    g_top = rmax(g_prob)
    g_idx = first_lane(is_g & (g_prob == g_top))

    e_lane = lane - N_GROUPS
    in_g = (e_lane >= 0) & (e_lane < N_EXPERTS) & ((e_lane // EXPERTS_PER_GROUP) == g_idx)
    e_exp = jnp.where(in_g, jnp.exp(lg - rmax(jnp.where(in_g, lg, NEG_BIG))), 0.0)
    e_prob = e_exp / rsum(e_exp)
    p1 = rmax(jnp.where(in_g, e_prob, -1.0))
    i1 = first_lane(in_g & (e_prob == p1))
    rest = in_g & (lane != i1)
    p2 = rmax(jnp.where(rest, e_prob, -1.0))
    i2 = first_lane(rest & (e_prob == p2))
    w1 = g_top * (p1 / (p1 + p2))
    w2 = g_top * (p2 / (p1 + p2))
    ids = jnp.where(lane == 0, i1 - N_GROUPS, i2 - N_GROUPS).astype(F32)
    tm = lg.shape[0]
    hit1, hit2 = lane == i1, lane == i2
    onehot = jnp.where(hit1 | hit2, 1.0, 0.0)
    earlier = jnp.where(lax.broadcasted_iota(jnp.int32, (tm, tm), 1) < lax.broadcasted_iota(jnp.int32, (tm, tm), 0),
                        1.0, 0.0).astype(BF16)
    before = jnp.dot(earlier, onehot.astype(BF16), preferred_element_type=F32) + counts
    r1 = rsum(jnp.where(hit1, before, 0.0))
    r2 = rsum(jnp.where(hit2, before, 0.0))
    vals = (ids, ids, w1, w2, r1, r2)
    routing = jnp.zeros(lg.shape, F32)
    for i, v in enumerate(vals):
        routing = jnp.where(lane == i, v, routing)
    return routing, counts + jnp.sum(onehot, axis=0, keepdims=True)


def mixproj(xa, pm, gdn_f, gdn_b, gla_f, gla_b, d_lat, d_ctx, w_out, gate, gdn_gain, gla_gain, sc_w, norm2,
            shift, scale, w_router, b_router, n_lat, n_ctx):
    w_hi = w_router.astype(BF16)
    w_lo = (w_router - w_hi.astype(F32)).astype(BF16)
    rows = xa.shape[0]
    n_tok = n_lat + n_ctx
    tm = MIX_TM
    assert n_lat % tm == 0 and n_ctx == tm
    lat_tiles = n_lat // tm
    per = tm // HALO
    last_halo = rows // HALO - 1
    seg = lambda name: pl.BlockSpec((tm, GROUP_WIDTH), lambda i, c=MAIN_OFFSETS[name][0] // GROUP_WIDTH: (i, c))
    halo_prev = lambda name: pl.BlockSpec(
        (HALO, GROUP_WIDTH), lambda i, c=MAIN_OFFSETS[name][0] // GROUP_WIDTH: (jnp.maximum(i * per - 1, 0), c))
    halo_next = lambda name: pl.BlockSpec(
        (HALO, GROUP_WIDTH), lambda i, c=MAIN_OFFSETS[name][0] // GROUP_WIDTH: (jnp.minimum((i + 1) * per, last_halo), c))
    tok = pl.BlockSpec((tm, GROUP_WIDTH), lambda i: (i, 0))
    full = pl.BlockSpec((tm, D_MODEL), lambda i: (i, 0))
    mod = pl.BlockSpec((None, 1, D_MODEL), lambda i: (i // lat_tiles, 0, 0))
    const = lambda shape: pl.BlockSpec(shape, lambda i: (0,) * len(shape))
    return pl.pallas_call(
        functools.partial(_mixproj_body, n_lat=n_lat, n_tok=n_tok),
        grid=(n_tok // tm,),
        in_specs=[
            full, tok, tok, seg('gdn_gate'), seg('sc_b'), seg('sc_c'), seg('sc_x'),
            halo_prev('sc_c'), halo_next('sc_c'), halo_prev('sc_x'), halo_next('sc_x'),
            tok, tok, seg('gla_r'),
            pl.BlockSpec((tm, GROUP_WIDTH), lambda i: (jnp.minimum(i, lat_tiles - 1), 0)),
            pl.BlockSpec((tm, GROUP_WIDTH), lambda i: (0, 0)),
            const((D_MODEL, D_MODEL)), mod, const((1, GROUP_WIDTH)), const((1, GROUP_WIDTH)), const((3, GROUP_WIDTH)),
            const((1, D_MODEL)), mod, mod, const((D_MODEL, ROUTER_WIDTH)), const((D_MODEL, ROUTER_WIDTH)),
            const((1, ROUTER_WIDTH)),
        ],
        out_specs=[full, full, pl.BlockSpec((tm, ROUTER_WIDTH), lambda i: (i, 0)),
                   pl.BlockSpec((None, HALO, ROUTER_WIDTH), lambda i: (i, 0, 0))],
        out_shape=[
            jax.ShapeDtypeStruct((rows, D_MODEL), F32),
            jax.ShapeDtypeStruct((n_tok, D_MODEL), BF16),
            jax.ShapeDtypeStruct((n_tok, ROUTER_WIDTH), F32),
            jax.ShapeDtypeStruct((n_tok // tm, HALO, ROUTER_WIDTH), F32),
        ],
        scratch_shapes=[pltpu.VMEM((1, ROUTER_WIDTH), F32)],
        input_output_aliases={0: 0},
        compiler_params=pltpu.CompilerParams(dimension_semantics=("arbitrary",), vmem_limit_bytes=VMEM_LIMIT),
        name="mixproj",
    )(xa, gdn_f, gdn_b, pm, pm, pm, pm, pm, pm, pm, pm, gla_f, gla_b, pm, d_lat, d_ctx, w_out, gate,
      gdn_gain, gla_gain, sc_w, norm2, shift, scale, w_hi, w_lo, b_router)


def _ffn_body(be_ref, nu_ref, x_ref, w1_ref, w3_ref, w2_ref, o_ref, w1b_ref, w3b_ref, w2b_ref):
    b = pl.program_id(0)
    used = b < nu_ref[0]
    new_expert = jnp.logical_or(b == 0, be_ref[b] != be_ref[jnp.maximum(b - 1, 0)])

    @pl.when(jnp.logical_and(used, new_expert))
    def _():
        w1b_ref[...] = w1_ref[...].astype(BF16)
        w3b_ref[...] = w3_ref[...].astype(BF16)
        w2b_ref[...] = w2_ref[...].astype(BF16)

    @pl.when(used)
    def _():
        x = x_ref[...]
        a = jnp.dot(x, w1b_ref[...], preferred_element_type=F32)
        g = jnp.dot(x, w3b_ref[...], preferred_element_type=F32)
        mid = (a * jax.nn.sigmoid(a) * g).astype(BF16)
        o_ref[...] = jnp.dot(mid, w2b_ref[...], preferred_element_type=F32)

    @pl.when(jnp.logical_not(used))
    def _():
        o_ref[...] = jnp.zeros(o_ref.shape, F32)


def expert_ffn(block_expert, n_used, xs, w1, w3, w2, layer):
    L = xs.shape[0]
    n_blocks = L // EXPERT_BLOCK
    wspec = lambda shape: pl.BlockSpec((None, None) + shape, lambda b, be, nu: (layer, be[b], 0, 0))
    return pl.pallas_call(
        _ffn_body,
        grid_spec=pltpu.PrefetchScalarGridSpec(
            num_scalar_prefetch=2,
            grid=(n_blocks,),
            in_specs=[
                pl.BlockSpec((EXPERT_BLOCK, D_MODEL), lambda b, be, nu: (b, 0)),
                wspec((D_MODEL, D_EXPERT)), wspec((D_MODEL, D_EXPERT)), wspec((D_EXPERT, D_MODEL)),
            ],
            out_specs=pl.BlockSpec((EXPERT_BLOCK, D_MODEL), lambda b, be, nu: (b, 0)),
            scratch_shapes=[
                pltpu.VMEM((D_MODEL, D_EXPERT), BF16),
                pltpu.VMEM((D_MODEL, D_EXPERT), BF16),
                pltpu.VMEM((D_EXPERT, D_MODEL), BF16),
            ],
        ),
        out_shape=jax.ShapeDtypeStruct((L, D_MODEL), F32),
        compiler_params=pltpu.CompilerParams(dimension_semantics=("arbitrary",), vmem_limit_bytes=VMEM_LIMIT),
        name="expert_ffn",
    )(block_expert, n_used, xs, w1, w3, w2)


def hier_moe_pallas(hb, routing, counts, w1, w3, w2, layer):
    N, D = hb.shape
    E, M, K = N_EXPERTS, EXPERT_BLOCK, TOP_K
    expert_idx = routing[:, 0:K].astype(jnp.int32)
    weights = routing[:, K:2 * K]
    rank = routing[:, 2 * K:3 * K].astype(jnp.int32)
    counts = counts[N_GROUPS:N_GROUPS + E].astype(jnp.int32)
    padded = (counts + M - 1) // M * M
    pad_end = jnp.cumsum(padded)
    pad_start = pad_end - padded
    pos = pad_start[expert_idx] + rank
    n_blocks = -(-(N * K) // M) + E
    L = n_blocks * M
    token = jnp.broadcast_to(jnp.arange(N, dtype=jnp.int32)[:, None], (N, K))
    slot_tok = jnp.full((L,), N, jnp.int32).at[pos.reshape(-1)].set(token.reshape(-1))
    block_expert = jnp.minimum(
        jnp.searchsorted(pad_end, jnp.arange(n_blocks, dtype=jnp.int32) * M, side='right'), E - 1).astype(jnp.int32)
    n_used = (pad_end[-1] // M).astype(jnp.int32).reshape(1)
    h_pad = jnp.concatenate([hb, jnp.zeros((1, D), BF16)], axis=0)
    xs = h_pad[slot_tok]
    y = expert_ffn(block_expert, n_used, xs, w1, w3, w2, layer)
    return y[pos[:, 0]] * weights[:, 0:1] + y[pos[:, 1]] * weights[:, 1:2]


def _prep_w_in(w):
    cols = lambda names: jnp.concatenate(
        [w[:, PROJ_OFFSETS[n][0]:PROJ_OFFSETS[n][0] + PROJ_OFFSETS[n][1]] for n in names], axis=1)
    small = cols(SMALL_GROUPS)
    small = jnp.pad(small, ((0, 0), (0, SMALL_WIDTH - small.shape[1])))
    return cols(MAIN_GROUPS).astype(BF16), small.astype(BF16)


def kernel(x, c, ctx, c_ctx, w_mod, b_mod, norm1, norm2, w_in, w_out, gdn_conv, gdn_a_log, gdn_dt_bias, gdn_out_norm, sc_conv, gla_gate_up, gla_gate_bias, gla_out_norm, diff_q_norm, diff_k_norm, diff_lambda, diff_out_norm, router_group, router_group_bias, router_expert, router_expert_bias, expert_w1, expert_w3, expert_w2):
    n_lat, n_ctx = x.shape[1], ctx.shape[1]
    n_tok = n_lat + n_ctx
    rows = -(-n_tok // ROW_TILE) * ROW_TILE
    pad_rows = lambda t: jnp.pad(t, ((0, rows - t.shape[0]), (0, 0)))
    cos, sin = rope_tables(n_lat, rows)
    is_lat = jnp.arange(rows, dtype=jnp.int32)[:, None] < n_lat

    xa = pad_rows(jnp.concatenate([x[0], ctx[0]], axis=0))
    cond, cond_ctx = jax.nn.silu(c), jax.nn.silu(c_ctx)
    for l in range(DEPTH):
        with_ctx_out = l < DEPTH - 1
        lam_init = 0.8 - 0.6 * math.exp(-0.3 * l)
        m = jnp.split(cond @ w_mod[l] + b_mod[l], 6, axis=-1)
        mc = jnp.split((cond_ctx @ w_mod[l] + b_mod[l])[None, :], 6, axis=-1)
        mod = [jnp.stack([m[i], mc[i]], axis=0) for i in range(6)]

        w_main, w_small = _prep_w_in(w_in[l])
        pm, ps = inproj(xa, norm1[l][None, :], mod[0], mod[1], w_main, w_small, n_lat)

        neg_a, dt = gdn_gate_vectors(gdn_a_log[l], gdn_dt_bias[l])
        gq, gk, gv, gs = gdn_prep(pm, ps, gdn_conv[l], neg_a, dt, n_lat, n_tok)
        gdn_f, gdn_b = gdn_scan(gq, gk, gv, gs, n_lat, n_ctx)

        w_up, up_bias = gla_gate_weights(gla_gate_up[l], gla_gate_bias[l])
        gla_f, gla_b = gla_scan(pm, ps, w_up, up_bias, n_lat, n_ctx)

        lam = diff_lambda_scalar(diff_lambda[l], lam_init)
        q_gain = jnp.tile(diff_q_norm[l], 2 * DIFF_HEADS)[None, :]
        k_gain = jnp.tile(diff_k_norm[l], 2 * DIFF_HEADS)[None, :]
        d_gain = diff_out_norm[l][:, None]
        dq, dk, dvt = attn_prep(pm, cos, sin, q_gain, k_gain)
        d_lat = diff_attention(lam, dq, dk, dvt, d_gain, 1.0 - lam_init, n_lat, n_ctx, False)
        if with_ctx_out:
            d_ctx = diff_attention(lam, dq, dk, dvt, d_gain, 1.0 - lam_init, n_lat, n_ctx, True)
        else:
            d_ctx = jnp.zeros((n_ctx, GROUP_WIDTH), F32)

        w_router = jnp.pad(jnp.concatenate([router_group[l], router_expert[l]], axis=1),
                           ((0, 0), (0, ROUTER_WIDTH - N_GROUPS - N_EXPERTS)))
        pad_lanes = lambda t: jnp.pad(t, ((0, 0), (0, ROUTER_WIDTH - t.shape[1])))
        b_router = pad_lanes(jnp.concatenate([router_group_bias[l], router_expert_bias[l]])[None, :])
        xa, h2, routing, counts = mixproj(
            xa, pm, gdn_f, gdn_b, gla_f, gla_b, d_lat, d_ctx, w_out[l].astype(BF16), mod[2],
            jnp.tile(gdn_out_norm[l], GDN_HEADS)[None, :], jnp.tile(gla_out_norm[l], GLA_HEADS)[None, :],
            sc_conv[l], norm2[l][None, :], mod[3], mod[4], w_router, b_router, n_lat, n_ctx)

        n_moe = n_tok if with_ctx_out else n_lat
        y = hier_moe_pallas(h2[:n_moe], routing[:n_moe], counts[n_moe // MIX_TM - 1, 0],
                            expert_w1, expert_w3, expert_w2, l)
        xa = xa + jnp.where(is_lat, m[5], mc[5]) * pad_rows(y)
    return xa[:n_lat][None]
```

```python
import functools
import math

import jax
import jax.numpy as jnp
from jax import lax
from jax.experimental import pallas as pl
from jax.experimental.pallas import tpu as pltpu

D_MODEL = 2048
DEPTH = 2
GRID_W = 64
GROUP_WIDTH = 512
GDN_DK = 64
GDN_DV = 64
GDN_HEADS = GROUP_WIDTH // GDN_DV
SCAN_CHUNK = 64
SC_WIDTH = GROUP_WIDTH
GLA_DK = 64
GLA_DV = 128
GLA_HEADS = GROUP_WIDTH // GLA_DV
GLA_GATE_RANK = 16
GLA_TAU = 16.0
DIFF_DK = 64
DIFF_DV = 128
DIFF_HEADS = GROUP_WIDTH // DIFF_DV
ROPE_BASE = 10000.0
ROPE_PAIRS = DIFF_DK // 4
N_GROUPS = 4
EXPERTS_PER_GROUP = 8
N_EXPERTS = N_GROUPS * EXPERTS_PER_GROUP
TOP_K = 2
D_EXPERT = D_MODEL // 4
EXPERT_BLOCK = 256
NORM_EPS = 1e-6
F32 = jnp.float32
BF16 = jnp.bfloat16
HI = lax.Precision.HIGHEST

PROJ_LAYOUT = (
    ('gdn_qkv', GDN_HEADS * (2 * GDN_DK + GDN_DV)),
    ('gdn_gate', GDN_HEADS * GDN_DV),
    ('gdn_beta', 2 * GDN_HEADS),
    ('gdn_alpha', 2 * GDN_HEADS),
    ('sc_b', SC_WIDTH),
    ('sc_c', SC_WIDTH),
    ('sc_x', SC_WIDTH),
    ('gla_q', GLA_HEADS * GLA_DK),
    ('gla_k', GLA_HEADS * GLA_DK),
    ('gla_v', GLA_HEADS * GLA_DV),
    ('gla_r', GLA_HEADS * GLA_DV),
    ('gla_lowrank', 2 * GLA_GATE_RANK),
    ('diff_q', DIFF_HEADS * 2 * DIFF_DK),
    ('diff_k', DIFF_HEADS * 2 * DIFF_DK),
    ('diff_v', DIFF_HEADS * DIFF_DV),
)
PROJ_OFFSETS = {}
_off = 0
for _name, _size in PROJ_LAYOUT:
    PROJ_OFFSETS[_name] = (_off, _size)
    _off += _size

MAIN_GROUPS = ('gdn_qkv', 'gdn_gate', 'sc_b', 'sc_c', 'sc_x', 'gla_q', 'gla_k', 'gla_v', 'gla_r',
               'diff_q', 'diff_k', 'diff_v')
SMALL_GROUPS = ('gdn_beta', 'gdn_alpha', 'gla_lowrank')
MAIN_OFFSETS = {}
_off = 0
for _name in MAIN_GROUPS:
    MAIN_OFFSETS[_name] = (_off, PROJ_OFFSETS[_name][1])
    _off += PROJ_OFFSETS[_name][1]
MAIN_WIDTH = _off
SMALL_OFFSETS = {}
_off = 0
for _name in SMALL_GROUPS:
    SMALL_OFFSETS[_name] = (_off, PROJ_OFFSETS[_name][1])
    _off += PROJ_OFFSETS[_name][1]
LANE = 128
SMALL_WIDTH = LANE

VMEM_LIMIT = 48 * 1024 * 1024
ROW_TILE = 1024


INPROJ_TM = 1024
INPROJ_TN = 512


def _inproj_body(x_ref, gain_ref, shift_ref, scale_ref, w_ref, ws_ref, o_ref, os_ref, h_ref):
    @pl.when(pl.program_id(1) == 0)
    def _():
        x = x_ref[...]
        y = x * lax.rsqrt(jnp.mean(x * x, axis=-1, keepdims=True) + NORM_EPS) * gain_ref[...]
        hb = (y * (1.0 + scale_ref[...]) + shift_ref[...]).astype(BF16)
        h_ref[...] = hb
        os_ref[...] = jnp.dot(hb, ws_ref[...], preferred_element_type=F32)

    o_ref[...] = jnp.dot(h_ref[...], w_ref[...], preferred_element_type=F32)


def inproj(x, gain, shift, scale, w_main, w_small, n_lat):
    rows = x.shape[0]
    tm = INPROJ_TM
    lat_tiles = n_lat // tm
    mod = pl.BlockSpec((None, 1, D_MODEL), lambda i, j: (i // lat_tiles, 0, 0))
    return pl.pallas_call(
        _inproj_body,
        grid=(rows // tm, MAIN_WIDTH // INPROJ_TN),
        in_specs=[
            pl.BlockSpec((tm, D_MODEL), lambda i, j: (i, 0)),
            pl.BlockSpec((1, D_MODEL), lambda i, j: (0, 0)),
            mod, mod,
            pl.BlockSpec((D_MODEL, INPROJ_TN), lambda i, j: (0, j)),
            pl.BlockSpec((D_MODEL, SMALL_WIDTH), lambda i, j: (0, 0)),
        ],
        out_specs=[
            pl.BlockSpec((tm, INPROJ_TN), lambda i, j: (i, j)),
            pl.BlockSpec((tm, SMALL_WIDTH), lambda i, j: (i, 0)),
        ],
        out_shape=[
            jax.ShapeDtypeStruct((rows, MAIN_WIDTH), F32),
            jax.ShapeDtypeStruct((rows, SMALL_WIDTH), F32),
        ],
        scratch_shapes=[pltpu.VMEM((tm, D_MODEL), BF16)],
        compiler_params=pltpu.CompilerParams(
            dimension_semantics=("parallel", "arbitrary"), vmem_limit_bytes=VMEM_LIMIT),
        name="inproj",
    )(x, gain, shift, scale, w_main, w_small)


HEAD_LANES = 2 * DIFF_DK
ATTN_TK = 512
ATTN_TQ = 1024
BF16_SUBLANES = 16
DV_EXT = DIFF_DV + BF16_SUBLANES
LOG2E = math.log2(math.e)
NEG_BIG = -1e30


def _group_mean_sq(x, width):
    n = x.shape[-1]
    gi = lax.broadcasted_iota(jnp.int32, (n, n), 0) // width
    gj = lax.broadcasted_iota(jnp.int32, (n, n), 1) // width
    ones = jnp.where(gi == gj, 1.0, 0.0).astype(BF16)
    rest = x * x
    total = None
    for _ in range(3):
        term = rest.astype(BF16)
        part = jnp.dot(term, ones, preferred_element_type=F32)
        total = part if total is None else total + part
        rest = rest - term.astype(F32)
    return total * (1.0 / width)


def _rope_swap(x):
    n = x.shape[-1]
    lane = lax.broadcasted_iota(jnp.int32, x.shape, x.ndim - 1)
    return jnp.where((lane % (2 * ROPE_PAIRS)) < ROPE_PAIRS,
                     pltpu.roll(x, n - ROPE_PAIRS, 1), pltpu.roll(x, ROPE_PAIRS, 1))


def _attn_prep_body(q_ref, k_ref, v_ref, cos_ref, sin_ref, qg_ref, kg_ref, qo_ref, ko_ref, vt_ref):
    cos = jnp.concatenate([cos_ref[...]] * DIFF_HEADS, axis=-1)
    sin = jnp.concatenate([sin_ref[...]] * DIFF_HEADS, axis=-1)

    def norm_rope(x, gain):
        y = x * lax.rsqrt(_group_mean_sq(x, DIFF_DK) + NORM_EPS) * gain
        return y * cos + _rope_swap(y) * sin

    q = norm_rope(q_ref[...], qg_ref[...]) * (DIFF_DK ** -0.5 * LOG2E)
    lane = lax.broadcasted_iota(jnp.int32, q.shape, 1)
    first = (lane % HEAD_LANES) < DIFF_DK
    qo_ref[0] = jnp.where(first, q, 0.0).astype(BF16)
    qo_ref[1] = jnp.where(first, 0.0, q).astype(BF16)
    ko_ref[...] = norm_rope(k_ref[...], kg_ref[...]).astype(BF16)
    v = v_ref[...]
    ones_tile = jnp.where(lax.broadcasted_iota(jnp.int32, (BF16_SUBLANES, v.shape[0]), 0) == 0, 1.0, 0.0)
    for h in range(DIFF_HEADS):
        vt_ref[h, 0] = jnp.concatenate([v[:, h * DIFF_DV:(h + 1) * DIFF_DV].T, ones_tile], axis=0).astype(BF16)


def attn_prep(p_main, cos, sin, q_gain, k_gain):
    rows = p_main.shape[0]
    tm = ATTN_TK
    nb = rows // tm
    col = lambda name: MAIN_OFFSETS[name][0] // GROUP_WIDTH
    seg = lambda name: pl.BlockSpec((tm, GROUP_WIDTH), lambda i, c=col(name): (i, c))
    tab = pl.BlockSpec((tm, HEAD_LANES), lambda i: (i, 0))
    vec = pl.BlockSpec((1, GROUP_WIDTH), lambda i: (0, 0))
    return pl.pallas_call(
        _attn_prep_body,
        grid=(nb,),
        in_specs=[seg('diff_q'), seg('diff_k'), seg('diff_v'), tab, tab, vec, vec],
        out_specs=[
            pl.BlockSpec((2, tm, GROUP_WIDTH), lambda i: (0, i, 0)),
            pl.BlockSpec((tm, GROUP_WIDTH), lambda i: (i, 0)),
            pl.BlockSpec((DIFF_HEADS, 1, DV_EXT, tm), lambda i: (0, i, 0, 0)),
        ],
        out_shape=[
            jax.ShapeDtypeStruct((2, rows, GROUP_WIDTH), BF16),
            jax.ShapeDtypeStruct((rows, GROUP_WIDTH), BF16),
            jax.ShapeDtypeStruct((DIFF_HEADS, nb, DV_EXT, tm), BF16),
        ],
        compiler_params=pltpu.CompilerParams(dimension_semantics=("parallel",), vmem_limit_bytes=VMEM_LIMIT),
        name="attn_prep",
    )(p_main, p_main, p_main, cos, sin, q_gain, k_gain)


def _attn_body(lam_ref, q_ref, k_ref, vt_ref, gn_ref, o_ref, m_ref, acc_ref, st_ref, p_ref, *,
               n_lat_chunks, n_ctx, tq, post_scale):
    tk = ATTN_TK
    q2 = q_ref[...].reshape(2 * tq, HEAD_LANES)
    m_ref[...] = jnp.full(m_ref.shape, NEG_BIG, F32)
    acc_ref[...] = jnp.zeros(acc_ref.shape, F32)

    def scores(kc):
        return lax.dot_general(kc, q2, (((1,), (1,)), ((), ())), preferred_element_type=F32)

    def softmax_update(st, pv):
        m_prev = m_ref[...]
        m_new = jnp.maximum(m_prev, jnp.max(st, axis=0, keepdims=True))
        alpha = jnp.exp2(m_prev - m_new)
        acc_ref[...] = alpha * (acc_ref[...] + pv)
        m_ref[...] = m_new
        return jnp.exp2(st - m_new).astype(BF16)

    ctx0 = n_lat_chunks * tk
    p_ctx = softmax_update(scores(k_ref[ctx0:ctx0 + n_ctx, :]), 0.0)
    acc_ref[...] = acc_ref[...] + jnp.dot(vt_ref[n_lat_chunks][:, :n_ctx], p_ctx, preferred_element_type=F32)

    if n_lat_chunks:
        st_ref[0] = scores(k_ref[0:tk, :])
        p_ref[1] = jnp.zeros(p_ref.shape[1:], BF16)

        def half(j, cur, nxt):
            jn = jnp.minimum(j + 1, n_lat_chunks - 1)
            st_ref[nxt] = scores(k_ref[pl.ds(pl.multiple_of(jn * tk, tk), tk), :])
            pv = jnp.dot(vt_ref[jnp.maximum(j - 1, 0)], p_ref[nxt], preferred_element_type=F32)
            p_ref[cur] = softmax_update(st_ref[cur], pv)

        def body(i, carry):
            half(2 * i, 0, 1)
            half(2 * i + 1, 1, 0)
            return carry
        lax.fori_loop(0, n_lat_chunks // 2, body, 0)
        acc_ref[...] = acc_ref[...] + jnp.dot(vt_ref[n_lat_chunks - 1], p_ref[1], preferred_element_type=F32)

    acc = acc_ref[...]
    l = acc[DIFF_DV:DIFF_DV + 1]
    o = acc[:DIFF_DV, :tq] / l[:, :tq] - lam_ref[0, 0] * (acc[:DIFF_DV, tq:] / l[:, tq:])
    y = o * lax.rsqrt(jnp.mean(o * o, axis=0, keepdims=True) + NORM_EPS) * (gn_ref[...] * post_scale)
    o_ref[...] = y.T


def diff_attention(lam, q, k, vt, out_gain, post_scale, n_lat, n_ctx, context_queries):
    lat_chunks = n_lat // ATTN_TK
    if context_queries:
        rows, tq, q_off, n_lat_chunks = n_ctx, n_ctx, n_lat // n_ctx, 0
        k_spec = pl.BlockSpec((ATTN_TK, HEAD_LANES), lambda h, i: (lat_chunks, h))
        vt_spec = pl.BlockSpec((None, 1, DV_EXT, ATTN_TK), lambda h, i: (h, lat_chunks, 0, 0))
    else:
        assert lat_chunks % 2 == 0 and n_lat % ATTN_TQ == 0
        rows, tq, q_off, n_lat_chunks = n_lat, ATTN_TQ, 0, lat_chunks
        k_spec = pl.BlockSpec(((lat_chunks + 1) * ATTN_TK, HEAD_LANES), lambda h, i: (0, h))
        vt_spec = pl.BlockSpec((None, lat_chunks + 1, DV_EXT, ATTN_TK), lambda h, i: (h, 0, 0, 0))
    return pl.pallas_call(
        functools.partial(_attn_body, n_lat_chunks=n_lat_chunks, n_ctx=n_ctx, tq=tq, post_scale=post_scale),
        grid=(DIFF_HEADS, rows // tq),
        in_specs=[
            pl.BlockSpec(memory_space=pltpu.SMEM),
            pl.BlockSpec((2, tq, HEAD_LANES), lambda h, i: (0, q_off + i, h)),
            k_spec, vt_spec,
            pl.BlockSpec((DIFF_DV, 1), lambda h, i: (0, 0)),
        ],
        out_specs=pl.BlockSpec((tq, DIFF_DV), lambda h, i: (i, h)),
        out_shape=jax.ShapeDtypeStruct((rows, GROUP_WIDTH), F32),
        scratch_shapes=[
            pltpu.VMEM((1, 2 * tq), F32),
            pltpu.VMEM((DV_EXT, 2 * tq), F32),
            pltpu.VMEM((2, ATTN_TK, 2 * tq), F32),
            pltpu.VMEM((2, ATTN_TK, 2 * tq), BF16),
        ],
        compiler_params=pltpu.CompilerParams(
            dimension_semantics=("parallel", "arbitrary"), vmem_limit_bytes=VMEM_LIMIT),
        name="diff_attn_ctx" if context_queries else "diff_attn_lat",
    )(lam, q, k, vt, out_gain)


def rope_tables(n_lat, rows):
    pos = jnp.arange(n_lat, dtype=jnp.int32)
    row = (pos // GRID_W).astype(F32)
    col = (pos % GRID_W).astype(F32)
    inv_freq = ROPE_BASE ** (-jnp.arange(ROPE_PAIRS, dtype=F32) / ROPE_PAIRS)
    ang_r, ang_c = row[:, None] * inv_freq, col[:, None] * inv_freq
    cos = jnp.concatenate([jnp.cos(ang_r), jnp.cos(ang_r), jnp.cos(ang_c), jnp.cos(ang_c)], axis=-1)
    sin = jnp.concatenate([-jnp.sin(ang_r), jnp.sin(ang_r), -jnp.sin(ang_c), jnp.sin(ang_c)], axis=-1)
    cos, sin = jnp.tile(cos, (1, 2)), jnp.tile(sin, (1, 2))
    pad = ((0, rows - n_lat), (0, 0))
    return jnp.pad(cos, pad, constant_values=1.0), jnp.pad(sin, pad)


def diff_lambda_scalar(lam_params, lam_init):
    lq1, lk1, lq2, lk2 = lam_params
    return (jnp.exp(jnp.sum(lq1 * lk1)) - jnp.exp(jnp.sum(lq2 * lk2)) + lam_init).reshape(1, 1)


GDN_QKV_WIDTH = GDN_HEADS * (2 * GDN_DK + GDN_DV)
GDN_PREP_TM = 512
HALO = 8


def _softplus(z):
    return jnp.maximum(z, 0.0) + jnp.log(1.0 + jnp.exp(-jnp.abs(z)))


def _conv3(x, prev_row, next_row, w, row0, n_lat, n_tok):
    tm = x.shape[0]
    r = lax.broadcasted_iota(jnp.int32, (tm, 1), 0)
    rows = row0 + r
    xm1 = jnp.where(r == 0, prev_row, pltpu.roll(x, 1, 0))
    xm1 = jnp.where((rows == 0) | (rows == n_lat), 0.0, xm1)
    xp1 = jnp.where(r == tm - 1, next_row, pltpu.roll(x, tm - 1, 0))
    xp1 = jnp.where((rows == n_lat - 1) | (rows == n_tok - 1), 0.0, xp1)
    return xm1 * w[0:1] + x * w[1:2] + xp1 * w[2:3]


def _gdn_prep_body(x_ref, xp_ref, xn_ref, ps_ref, cw_ref, a_ref, dt_ref, q_ref, k_ref, v_ref, gs_ref, *, n_lat, n_tok):
    tm = x_ref.shape[0]
    y = _conv3(x_ref[...], xp_ref[HALO - 1:HALO, :], xn_ref[0:1, :], cw_ref[...], pl.program_id(0) * tm, n_lat, n_tok)
    y = y * jax.nn.sigmoid(y)
    hk = GDN_HEADS * GDN_DK
    q, k = y[:, :hk], y[:, hk:2 * hk]
    q_ref[...] = q * lax.rsqrt(_group_mean_sq(q, GDN_DK) * GDN_DK + NORM_EPS) * (GDN_DK ** -0.5)
    k_ref[...] = k * lax.rsqrt(_group_mean_sq(k, GDN_DK) * GDN_DK + NORM_EPS)
    v_ref[...] = y[:, 2 * hk:]
    ps = ps_ref[...]
    lane = lax.broadcasted_iota(jnp.int32, ps.shape, 1)
    gs_ref[...] = jnp.where(lane < 2 * GDN_HEADS, jax.nn.sigmoid(ps), a_ref[...] * _softplus(ps + dt_ref[...]))


def gdn_prep(p_main, p_small, conv_w, neg_a, dt, n_lat, n_tok):
    rows = p_main.shape[0]
    tm = GDN_PREP_TM
    nb = rows // tm
    per = tm // HALO
    out = jax.ShapeDtypeStruct((rows, GROUP_WIDTH), F32)
    return pl.pallas_call(
        functools.partial(_gdn_prep_body, n_lat=n_lat, n_tok=n_tok),
        grid=(nb,),
        in_specs=[
            pl.BlockSpec((tm, GDN_QKV_WIDTH), lambda i: (i, 0)),
            pl.BlockSpec((HALO, GDN_QKV_WIDTH), lambda i: (jnp.maximum(i * per - 1, 0), 0)),
            pl.BlockSpec((HALO, GDN_QKV_WIDTH), lambda i: (jnp.minimum((i + 1) * per, nb * per - 1), 0)),
            pl.BlockSpec((tm, SMALL_WIDTH), lambda i: (i, 0)),
            pl.BlockSpec((3, GDN_QKV_WIDTH), lambda i: (0, 0)),
            pl.BlockSpec((1, SMALL_WIDTH), lambda i: (0, 0)),
            pl.BlockSpec((1, SMALL_WIDTH), lambda i: (0, 0)),
        ],
        out_specs=[pl.BlockSpec((tm, GROUP_WIDTH), lambda i: (i, 0))] * 3 + [pl.BlockSpec((tm, SMALL_WIDTH), lambda i: (i, 0))],
        out_shape=[out, out, out, jax.ShapeDtypeStruct((rows, SMALL_WIDTH), F32)],
        compiler_params=pltpu.CompilerParams(dimension_semantics=("parallel",), vmem_limit_bytes=VMEM_LIMIT),
        name="gdn_prep",
    )(p_main, p_main, p_main, p_small, conv_w, neg_a, dt)


PAIR = 2 * SCAN_CHUNK
GDN_STEP_CHUNKS = 2


def _stack_pair(a):
    first = lax.broadcasted_iota(jnp.int32, a.shape, 1) < (PAIR // 2)
    return jnp.concatenate([jnp.where(first, a, 0.0), jnp.where(first, 0.0, a)], axis=0)


def _pair_col(arr, c0, c1):
    return jnp.concatenate([arr[:, c0:c0 + 1], arr[:, c1:c1 + 1]], axis=0)


def _dot_nt(a, b):
    return lax.dot_general(a, b, (((1,), (1,)), ((), ())), preferred_element_type=F32)


def _dot_tn(a, b):
    return lax.dot_general(a, b, (((0,), (0,)), ((), ())), preferred_element_type=F32)


def _chunk_masks(d):
    C = SCAN_CHUNK
    r = lax.broadcasted_iota(jnp.int32, (PAIR, PAIR), 0)
    c = lax.broadcasted_iota(jnp.int32, (PAIR, PAIR), 1)
    same = (r // C) == (c // C)
    ri, ci = r % C, c % C
    incl = same & ((ci <= ri) if d == 0 else (ci >= ri))
    strict = same & ((ci < ri) if d == 0 else (ci > ri))
    r1 = lax.broadcasted_iota(jnp.int32, (C, C), 0)
    c1 = lax.broadcasted_iota(jnp.int32, (C, C), 1)
    tri = jnp.where((c1 <= r1) if d == 0 else (c1 >= r1), 1.0, 0.0).astype(F32)
    return incl, strict, tri, r == c


def _gdn_scan_body(qf, kf, vf, gf, qb, kb, vb, gb, of_ref, ob_ref, s_ref):
    C = SCAN_CHUNK
    dot = functools.partial(jnp.dot, preferred_element_type=F32)

    @pl.when(pl.program_id(0) == 0)
    def _():
        s_ref[...] = jnp.zeros(s_ref.shape, F32)

    top = lax.broadcasted_iota(jnp.int32, (PAIR, 1), 0) < C
    units = []
    for d, (q_ref, k_ref, v_ref, g_ref, o_ref) in enumerate(((qf, kf, vf, gf, of_ref), (qb, kb, vb, gb, ob_ref))):
        incl, strict, tri, eye = _chunk_masks(d)
        for visit in range(GDN_STEP_CHUNKS):
            c = visit if d == 0 else GDN_STEP_CHUNKS - 1 - visit
            rs = slice(C * c, C * (c + 1))
            gs = g_ref[rs, :]
            g_all = jnp.dot(tri, gs, preferred_element_type=F32, precision=HI)
            g_end = g_all[C - 1:C] if d == 0 else g_all[0:1]
            g_t = g_all.T
            for p in range(GDN_HEADS // 2):
                sl = slice(PAIR * p, PAIR * (p + 1))
                b0 = GDN_HEADS * d + 2 * p
                c0 = 2 * GDN_HEADS + b0
                u = dict(d=d, p=p, sl=sl, rs=rs, visit=visit, o_ref=o_ref, incl=incl, strict=strict, eye=eye)
                u['K'], u['Q'], u['V'] = (_stack_pair(r[rs, sl]) for r in (k_ref, q_ref, v_ref))
                u['g_col'] = _pair_col(g_all, c0, c0 + 1)
                u['beta'] = _pair_col(gs, b0, b0 + 1)
                u['g_row'] = jnp.concatenate([g_t[c0:c0 + 1, :], g_t[c0 + 1:c0 + 2, :]], axis=1)
                u['gl_col'] = jnp.where(top, g_end[:, c0:c0 + 1], g_end[:, c0 + 1:c0 + 2])
                units.append(u)
    for u in units:
        u['decay'] = jnp.exp(jnp.where(u['incl'], u['g_col'] - u['g_row'], NEG_BIG))
        u['eg'] = jnp.exp(u['g_col'])
    for u in units:
        u['A'] = jnp.where(u['strict'], -(u['beta'] * _dot_nt(u['K'], u['K']) * u['decay']), 0.0)
        u['P'] = jnp.where(u['eye'], 1.0, 0.0) + u['A']
        u['negL'] = u['A']
    for u in units:
        u['attn'] = _dot_nt(u['Q'], u['K']) * u['decay']
    for _ in range(5):
        for u in units:
            u['A'] = dot(u['A'], u['A'])
        for u in units:
            u['P'] = u['P'] + dot(u['P'], u['A'])
    split = lambda t: (t.astype(BF16), (t - t.astype(BF16).astype(F32)).astype(BF16))
    for u in units:
        (l_hi, l_lo), (p_hi, p_lo) = split(u['negL']), split(u['P'])
        u['R'] = (jnp.where(u['eye'], 1.0, 0.0) - u['P']) + (dot(l_hi, p_hi) + (dot(l_hi, p_lo) + dot(l_lo, p_hi)))
    for u in units:
        u['P'] = u['P'] + dot(u['P'], u['R'])
    for u in units:
        u['uw'] = dot(u['P'], jnp.concatenate([u['V'] * u['beta'], u['K'] * (u['beta'] * u['eg'])], axis=1))
    state = {(d, p): s_ref[d, p] for d in range(2) for p in range(GDN_HEADS // 2)}
    for visit in range(GDN_STEP_CHUNKS):
        now = [u for u in units if u['visit'] == visit]
        for u in now:
            u['S'] = state[u['d'], u['p']]
            u['ws'] = dot(jnp.concatenate([u['uw'][:, PAIR:], u['Q'] * u['eg']], axis=0), u['S'])
        for u in now:
            u['v_new'] = u['uw'][:, :PAIR] - u['ws'][:PAIR]
            o_st = u['ws'][PAIR:] + dot(u['attn'], u['v_new'])
            u['o_ref'][u['rs'], u['sl']] = o_st[:C] + o_st[C:]
        for u in now:
            state[u['d'], u['p']] = (u['S'] * jnp.exp(u['gl_col'])
                                     + _dot_tn(u['K'] * jnp.exp(u['gl_col'] - u['g_col']), u['v_new']))
    for (d, p), s_new in state.items():
        s_ref[d, p] = s_new


def _scan_order(n_lat, n_ctx, block_rows=SCAN_CHUNK):
    assert n_lat % block_rows == 0 and n_ctx % block_rows == 0
    nl, nc = n_lat // block_rows, n_ctx // block_rows
    fwd = lambda t: (jnp.where(t < nc, nl + t, t - nc), 0)
    bwd = lambda t: (nl + nc - 1 - t, 0)
    return nl + nc, fwd, bwd


def gdn_scan(q, k, v, gs, n_lat, n_ctx):
    rows = GDN_STEP_CHUNKS * SCAN_CHUNK
    steps, fwd, bwd = _scan_order(n_lat, n_ctx, rows)
    wide = lambda f: pl.BlockSpec((rows, GROUP_WIDTH), f)
    small = lambda f: pl.BlockSpec((rows, SMALL_WIDTH), f)
    out = jax.ShapeDtypeStruct((n_lat + n_ctx, GROUP_WIDTH), F32)
    return pl.pallas_call(
        _gdn_scan_body,
        grid=(steps,),
        in_specs=[wide(fwd), wide(fwd), wide(fwd), small(fwd), wide(bwd), wide(bwd), wide(bwd), small(bwd)],
        out_specs=[wide(fwd), wide(bwd)],
        out_shape=[out, out],
        scratch_shapes=[pltpu.VMEM((2, GDN_HEADS // 2, PAIR, PAIR), F32)],
        compiler_params=pltpu.CompilerParams(dimension_semantics=("arbitrary",), vmem_limit_bytes=VMEM_LIMIT),
        name="gdn_scan",
    )(q, k, v, gs, q, k, v, gs)


def gdn_gate_vectors(a_log, dt_bias):
    n = 2 * GDN_HEADS
    pad = lambda t: jnp.pad(t.reshape(1, n), ((0, 0), (n, SMALL_WIDTH - 2 * n)))
    return pad(-jnp.exp(a_log)), pad(dt_bias)


GLA_QK_WIDTH = GLA_HEADS * GLA_DK


def _gla_scan_body(qf, kf, vf, pf, qb, kb, vb, pb, wup_ref, bias_ref, of_ref, ob_ref, s_ref):
    C = SCAN_CHUNK
    dot = functools.partial(jnp.dot, preferred_element_type=F32)

    @pl.when(pl.program_id(0) == 0)
    def _():
        s_ref[...] = jnp.zeros(s_ref.shape, F32)

    units = []
    for d, (q_ref, k_ref, v_ref, p_ref, o_ref) in enumerate(((qf, kf, vf, pf, of_ref), (qb, kb, vb, pb, ob_ref))):
        incl, _, tri, _ = _chunk_masks(d)
        z = dot(p_ref[...], wup_ref[...]) + bias_ref[...]
        z = z[:, GLA_QK_WIDTH * d:GLA_QK_WIDTH * (d + 1)]
        log_a = (jnp.minimum(z, 0.0) - jnp.log(1.0 + jnp.exp(-jnp.abs(z)))) * (1.0 / GLA_TAU)
        b = jnp.dot(tri, log_a, preferred_element_type=F32, precision=HI)
        b_end = b[C - 1:C] if d == 0 else b[0:1]
        q = q_ref[...] * (GLA_DK ** -0.5)
        k_out = k_ref[...] * jnp.exp(b_end - b)
        q_in = q * jnp.exp(b - b_end)
        q_st = q * jnp.exp(b)
        a_col = jnp.broadcast_to(jnp.exp(b_end), (C, GLA_QK_WIDTH)).T[:, 0:1]
        v = v_ref[...]
        for p in range(GLA_HEADS // 2):
            sl = slice(PAIR * p, PAIR * (p + 1))
            v0, v1 = v[:, 2 * p * GLA_DV:(2 * p + 1) * GLA_DV], v[:, (2 * p + 1) * GLA_DV:(2 * p + 2) * GLA_DV]
            units.append(dict(d=d, p=p, o_ref=o_ref, incl=incl, V=jnp.concatenate([v0, v1], axis=0),
                              K=_stack_pair(k_out[:, sl]), Qin=_stack_pair(q_in[:, sl]), Qst=_stack_pair(q_st[:, sl]),
                              a=a_col[sl]))
    for u in units:
        u['attn'] = jnp.where(u['incl'], _dot_nt(u['Qin'], u['K']), 0.0)
        u['S'] = s_ref[u['d'], u['p']]
    for u in units:
        u['o'] = dot(u['Qst'], u['S'])
        u['dS'] = _dot_tn(u['K'], u['V'])
    for u in units:
        o_st = u['o'] + dot(u['attn'], u['V'])
        p = u['p']
        u['o_ref'][:, 2 * p * GLA_DV:(2 * p + 1) * GLA_DV] = o_st[:C]
        u['o_ref'][:, (2 * p + 1) * GLA_DV:(2 * p + 2) * GLA_DV] = o_st[C:]
        s_ref[u['d'], p] = u['a'] * u['S'] + u['dS']


def gla_scan(p_main, p_small, w_up, bias, n_lat, n_ctx):
    steps, fwd, bwd = _scan_order(n_lat, n_ctx)
    qc, kc, vc = (MAIN_OFFSETS[n][0] for n in ('gla_q', 'gla_k', 'gla_v'))
    at = lambda f, blk: (lambda t: (f(t)[0], blk))
    qk = lambda f, off: pl.BlockSpec((SCAN_CHUNK, GLA_QK_WIDTH), at(f, off // GLA_QK_WIDTH))
    vv = lambda f: pl.BlockSpec((SCAN_CHUNK, GROUP_WIDTH), at(f, vc // GROUP_WIDTH))
    small = lambda f: pl.BlockSpec((SCAN_CHUNK, SMALL_WIDTH), f)
    const = lambda shape: pl.BlockSpec(shape, lambda t: (0, 0))
    out = jax.ShapeDtypeStruct((n_lat + n_ctx, GROUP_WIDTH), F32)
    return pl.pallas_call(
        _gla_scan_body,
        grid=(steps,),
        in_specs=[qk(fwd, qc), qk(fwd, kc), vv(fwd), small(fwd), qk(bwd, qc), qk(bwd, kc), vv(bwd), small(bwd),
                  const((SMALL_WIDTH, 2 * GLA_QK_WIDTH)), const((1, 2 * GLA_QK_WIDTH))],
        out_specs=[pl.BlockSpec((SCAN_CHUNK, GROUP_WIDTH), fwd), pl.BlockSpec((SCAN_CHUNK, GROUP_WIDTH), bwd)],
        out_shape=[out, out],
        scratch_shapes=[pltpu.VMEM((2, GLA_HEADS // 2, PAIR, GLA_DV), F32)],
        compiler_params=pltpu.CompilerParams(dimension_semantics=("arbitrary",), vmem_limit_bytes=VMEM_LIMIT),
        name="gla_scan",
    )(p_main, p_main, p_main, p_small, p_main, p_main, p_main, p_small, w_up, bias)


def gla_gate_weights(gate_up, gate_bias):
    off = SMALL_OFFSETS['gla_lowrank'][0]
    w = jnp.zeros((SMALL_WIDTH, 2 * GLA_QK_WIDTH), F32)
    for d in range(2):
        w = w.at[off + d * GLA_GATE_RANK:off + (d + 1) * GLA_GATE_RANK, d * GLA_QK_WIDTH:(d + 1) * GLA_QK_WIDTH].set(gate_up[d])
    return w, gate_bias.reshape(1, 2 * GLA_QK_WIDTH)


MIX_TM = 256
ROUTER_WIDTH = LANE


def _mixproj_body(x_ref, gf_ref, gb_ref, gate_ref, scb_ref, scc_ref, scx_ref, ccp_ref, ccn_ref, cxp_ref, cxn_ref,
                  lf_ref, lb_ref, r_ref, dl_ref, dc_ref, w_ref, g2_ref, gng_ref, gnl_ref, scw_ref, n2_ref,
                  shift_ref, scale_ref, wrh_ref, wrl_ref, rb_ref, xo_ref, h2_ref, rt_ref, co_ref, cnt_ref, *,
                  n_lat, n_tok):
    tm = x_ref.shape[0]
    row0 = pl.program_id(0) * tm
    silu = lambda t: t * jax.nn.sigmoid(t)
    dot = functools.partial(jnp.dot, preferred_element_type=F32)

    og = gf_ref[...] + gb_ref[...]
    a = og * lax.rsqrt(_group_mean_sq(og, GDN_DV) + NORM_EPS) * gng_ref[...] * silu(gate_ref[...])
    u = scc_ref[...] * scx_ref[...]
    u_prev = ccp_ref[HALO - 1:HALO, :] * cxp_ref[HALO - 1:HALO, :]
    u_next = ccn_ref[0:1, :] * cxn_ref[0:1, :]
    b = scb_ref[...] * _conv3(u, u_prev, u_next, scw_ref[...], row0, n_lat, n_tok)
    ol = lf_ref[...] + lb_ref[...]
    g = ol * lax.rsqrt(_group_mean_sq(ol, GLA_DV) + NORM_EPS) * gnl_ref[...] * silu(r_ref[...])
    d = jnp.where(row0 < n_lat, dl_ref[...], dc_ref[...])

    W = GROUP_WIDTH
    y = (dot(a.astype(BF16), w_ref[0:W, :]) + dot(b.astype(BF16), w_ref[W:2 * W, :])
         + dot(g.astype(BF16), w_ref[2 * W:3 * W, :]) + dot(d.astype(BF16), w_ref[3 * W:4 * W, :]))
    xn = x_ref[...] + g2_ref[...] * y
    xo_ref[...] = xn
    h2 = xn * lax.rsqrt(jnp.mean(xn * xn, axis=-1, keepdims=True) + NORM_EPS) * n2_ref[...]
    h2 = h2 * (1.0 + scale_ref[...]) + shift_ref[...]
    hb = h2.astype(BF16)
    h2_ref[...] = hb
    h_lo = (h2 - hb.astype(F32)).astype(BF16)
    logits = dot(hb, wrh_ref[...]) + (dot(h_lo, wrh_ref[...]) + dot(hb, wrl_ref[...])) + rb_ref[...]
    @pl.when(pl.program_id(0) == 0)
    def _():
        cnt_ref[...] = jnp.zeros(cnt_ref.shape, F32)

    routing, counts = _route(logits, cnt_ref[...])
    rt_ref[...] = routing
    cnt_ref[...] = counts
    co_ref[...] = jnp.broadcast_to(counts, co_ref.shape)


def _route(lg, counts):
    lane = lax.broadcasted_iota(jnp.int32, lg.shape, 1)
    big = 4 * LANE
    rmax = lambda t: jnp.max(t, axis=1, keepdims=True)
    rsum = lambda t: jnp.sum(t, axis=1, keepdims=True)
    first_lane = lambda cond: jnp.min(jnp.where(cond, lane, big), axis=1, keepdims=True)

    is_g = lane < N_GROUPS
    g_exp = jnp.where(is_g, jnp.exp(lg - rmax(jnp.where(is_g, lg, NEG_BIG))), 0.0)
    g_prob = g_exp / rsum(g_exp)
    g_top = rmax(g_prob)
    g_idx = first_lane(is_g & (g_prob == g_top))

    e_lane = lane - N_GROUPS
    in_g = (e_lane >= 0) & (e_lane < N_EXPERTS) & ((e_lane // EXPERTS_PER_GROUP) == g_idx)
    e_exp = jnp.where(in_g, jnp.exp(lg - rmax(jnp.where(in_g, lg, NEG_BIG))), 0.0)
    e_prob = e_exp / rsum(e_exp)
    p1 = rmax(jnp.where(in_g, e_prob, -1.0))
    i1 = first_lane(in_g & (e_prob == p1))
    rest = in_g & (lane != i1)
    p2 = rmax(jnp.where(rest, e_prob, -1.0))
    i2 = first_lane(rest & (e_prob == p2))
    w1 = g_top * (p1 / (p1 + p2))
    w2 = g_top * (p2 / (p1 + p2))
    ids = jnp.where(lane == 0, i1 - N_GROUPS, i2 - N_GROUPS).astype(F32)
    tm = lg.shape[0]
    hit1, hit2 = lane == i1, lane == i2
    onehot = jnp.where(hit1 | hit2, 1.0, 0.0)
    earlier = jnp.where(lax.broadcasted_iota(jnp.int32, (tm, tm), 1) < lax.broadcasted_iota(jnp.int32, (tm, tm), 0),
                        1.0, 0.0).astype(BF16)
    before = jnp.dot(earlier, onehot.astype(BF16), preferred_element_type=F32) + counts
    r1 = rsum(jnp.where(hit1, before, 0.0))
    r2 = rsum(jnp.where(hit2, before, 0.0))
    vals = (ids, ids, w1, w2, r1, r2)
    routing = jnp.zeros(lg.shape, F32)
    for i, v in enumerate(vals):
        routing = jnp.where(lane == i, v, routing)
    return routing, counts + jnp.sum(onehot, axis=0, keepdims=True)


def mixproj(xa, pm, gdn_f, gdn_b, gla_f, gla_b, d_lat, d_ctx, w_out, gate, gdn_gain, gla_gain, sc_w, norm2,
            shift, scale, w_router, b_router, n_lat, n_ctx):
    w_hi = w_router.astype(BF16)
    w_lo = (w_router - w_hi.astype(F32)).astype(BF16)
    rows = xa.shape[0]
    n_tok = n_lat + n_ctx
    tm = MIX_TM
    assert n_lat % tm == 0 and n_ctx == tm
    lat_tiles = n_lat // tm
    per = tm // HALO
    last_halo = rows // HALO - 1
    seg = lambda name: pl.BlockSpec((tm, GROUP_WIDTH), lambda i, c=MAIN_OFFSETS[name][0] // GROUP_WIDTH: (i, c))
    halo_prev = lambda name: pl.BlockSpec(
        (HALO, GROUP_WIDTH), lambda i, c=MAIN_OFFSETS[name][0] // GROUP_WIDTH: (jnp.maximum(i * per - 1, 0), c))
    halo_next = lambda name: pl.BlockSpec(
        (HALO, GROUP_WIDTH), lambda i, c=MAIN_OFFSETS[name][0] // GROUP_WIDTH: (jnp.minimum((i + 1) * per, last_halo), c))
    tok = pl.BlockSpec((tm, GROUP_WIDTH), lambda i: (i, 0))
    full = pl.BlockSpec((tm, D_MODEL), lambda i: (i, 0))
    mod = pl.BlockSpec((None, 1, D_MODEL), lambda i: (i // lat_tiles, 0, 0))
    const = lambda shape: pl.BlockSpec(shape, lambda i: (0,) * len(shape))
    return pl.pallas_call(
        functools.partial(_mixproj_body, n_lat=n_lat, n_tok=n_tok),
        grid=(n_tok // tm,),
        in_specs=[
            full, tok, tok, seg('gdn_gate'), seg('sc_b'), seg('sc_c'), seg('sc_x'),
            halo_prev('sc_c'), halo_next('sc_c'), halo_prev('sc_x'), halo_next('sc_x'),
            tok, tok, seg('gla_r'),
            pl.BlockSpec((tm, GROUP_WIDTH), lambda i: (jnp.minimum(i, lat_tiles - 1), 0)),
            pl.BlockSpec((tm, GROUP_WIDTH), lambda i: (0, 0)),
            const((D_MODEL, D_MODEL)), mod, const((1, GROUP_WIDTH)), const((1, GROUP_WIDTH)), const((3, GROUP_WIDTH)),
            const((1, D_MODEL)), mod, mod, const((D_MODEL, ROUTER_WIDTH)), const((D_MODEL, ROUTER_WIDTH)),
            const((1, ROUTER_WIDTH)),
        ],
        out_specs=[full, full, pl.BlockSpec((tm, ROUTER_WIDTH), lambda i: (i, 0)),
                   pl.BlockSpec((None, HALO, ROUTER_WIDTH), lambda i: (i, 0, 0))],
        out_shape=[
            jax.ShapeDtypeStruct((rows, D_MODEL), F32),
            jax.ShapeDtypeStruct((n_tok, D_MODEL), BF16),
            jax.ShapeDtypeStruct((n_tok, ROUTER_WIDTH), F32),
            jax.ShapeDtypeStruct((n_tok // tm, HALO, ROUTER_WIDTH), F32),
        ],
        scratch_shapes=[pltpu.VMEM((1, ROUTER_WIDTH), F32)],
        input_output_aliases={0: 0},
        compiler_params=pltpu.CompilerParams(dimension_semantics=("arbitrary",), vmem_limit_bytes=VMEM_LIMIT),
        name="mixproj",
    )(xa, gdn_f, gdn_b, pm, pm, pm, pm, pm, pm, pm, pm, gla_f, gla_b, pm, d_lat, d_ctx, w_out, gate,
      gdn_gain, gla_gain, sc_w, norm2, shift, scale, w_hi, w_lo, b_router)


def _ffn_body(be_ref, nu_ref, x_ref, w1_ref, w3_ref, w2_ref, o_ref, w1b_ref, w3b_ref, w2b_ref):
    b = pl.program_id(0)
    used = b < nu_ref[0]
    new_expert = jnp.logical_or(b == 0, be_ref[b] != be_ref[jnp.maximum(b - 1, 0)])

    @pl.when(jnp.logical_and(used, new_expert))
    def _():
        w1b_ref[...] = w1_ref[...].astype(BF16)
        w3b_ref[...] = w3_ref[...].astype(BF16)
        w2b_ref[...] = w2_ref[...].astype(BF16)

    @pl.when(used)
    def _():
        x = x_ref[...]
        a = jnp.dot(x, w1b_ref[...], preferred_element_type=F32)
        g = jnp.dot(x, w3b_ref[...], preferred_element_type=F32)
        mid = (a * jax.nn.sigmoid(a) * g).astype(BF16)
        o_ref[...] = jnp.dot(mid, w2b_ref[...], preferred_element_type=F32)

    @pl.when(jnp.logical_not(used))
    def _():
        o_ref[...] = jnp.zeros(o_ref.shape, F32)


def expert_ffn(block_expert, n_used, xs, w1, w3, w2, layer):
    L = xs.shape[0]
    n_blocks = L // EXPERT_BLOCK
    wspec = lambda shape: pl.BlockSpec((None, None) + shape, lambda b, be, nu: (layer, be[b], 0, 0))
    return pl.pallas_call(
        _ffn_body,
        grid_spec=pltpu.PrefetchScalarGridSpec(
            num_scalar_prefetch=2,
            grid=(n_blocks,),
            in_specs=[
                pl.BlockSpec((EXPERT_BLOCK, D_MODEL), lambda b, be, nu: (b, 0)),
                wspec((D_MODEL, D_EXPERT)), wspec((D_MODEL, D_EXPERT)), wspec((D_EXPERT, D_MODEL)),
            ],
            out_specs=pl.BlockSpec((EXPERT_BLOCK, D_MODEL), lambda b, be, nu: (b, 0)),
            scratch_shapes=[
                pltpu.VMEM((D_MODEL, D_EXPERT), BF16),
                pltpu.VMEM((D_MODEL, D_EXPERT), BF16),
                pltpu.VMEM((D_EXPERT, D_MODEL), BF16),
            ],
        ),
        out_shape=jax.ShapeDtypeStruct((L, D_MODEL), F32),
        compiler_params=pltpu.CompilerParams(dimension_semantics=("arbitrary",), vmem_limit_bytes=VMEM_LIMIT),
        name="expert_ffn",
    )(block_expert, n_used, xs, w1, w3, w2)


def hier_moe_pallas(hb, routing, counts, w1, w3, w2, layer):
    N, D = hb.shape
    E, M, K = N_EXPERTS, EXPERT_BLOCK, TOP_K
    expert_idx = routing[:, 0:K].astype(jnp.int32)
    weights = routing[:, K:2 * K]
    rank = routing[:, 2 * K:3 * K].astype(jnp.int32)
    counts = counts[N_GROUPS:N_GROUPS + E].astype(jnp.int32)
    padded = (counts + M - 1) // M * M
    pad_end = jnp.cumsum(padded)
    pad_start = pad_end - padded
    pos = pad_start[expert_idx] + rank
    n_blocks = -(-(N * K) // M) + E
    L = n_blocks * M
    token = jnp.broadcast_to(jnp.arange(N, dtype=jnp.int32)[:, None], (N, K))
    slot_tok = jnp.full((L,), N, jnp.int32).at[pos.reshape(-1)].set(token.reshape(-1))
    block_row0 = jnp.arange(n_blocks, dtype=jnp.int32)[:, None] * M
    block_expert = jnp.minimum(jnp.sum((pad_end[None, :] <= block_row0).astype(jnp.int32), axis=1), E - 1)
    n_used = (pad_end[-1] // M).astype(jnp.int32).reshape(1)
    h_pad = jnp.concatenate([hb, jnp.zeros((1, D), BF16)], axis=0)
    xs = h_pad[slot_tok]
    y = expert_ffn(block_expert, n_used, xs, w1, w3, w2, layer)
    return y[pos[:, 0]] * weights[:, 0:1] + y[pos[:, 1]] * weights[:, 1:2]


def _prep_w_in(w):
    cols = lambda names: jnp.concatenate(
        [w[:, PROJ_OFFSETS[n][0]:PROJ_OFFSETS[n][0] + PROJ_OFFSETS[n][1]] for n in names], axis=1)
    small = cols(SMALL_GROUPS)
    small = jnp.pad(small, ((0, 0), (0, SMALL_WIDTH - small.shape[1])))
    return cols(MAIN_GROUPS).astype(BF16), small.astype(BF16)


def kernel(x, c, ctx, c_ctx, w_mod, b_mod, norm1, norm2, w_in, w_out, gdn_conv, gdn_a_log, gdn_dt_bias, gdn_out_norm, sc_conv, gla_gate_up, gla_gate_bias, gla_out_norm, diff_q_norm, diff_k_norm, diff_lambda, diff_out_norm, router_group, router_group_bias, router_expert, router_expert_bias, expert_w1, expert_w3, expert_w2):
    n_lat, n_ctx = x.shape[1], ctx.shape[1]
    n_tok = n_lat + n_ctx
    rows = -(-n_tok // ROW_TILE) * ROW_TILE
    pad_rows = lambda t: jnp.pad(t, ((0, rows - t.shape[0]), (0, 0)))
    cos, sin = rope_tables(n_lat, rows)
    is_lat = jnp.arange(rows, dtype=jnp.int32)[:, None] < n_lat

    xa = pad_rows(jnp.concatenate([x[0], ctx[0]], axis=0))
    cond, cond_ctx = jax.nn.silu(c), jax.nn.silu(c_ctx)
    for l in range(DEPTH):
        with_ctx_out = l < DEPTH - 1
        lam_init = 0.8 - 0.6 * math.exp(-0.3 * l)
        m = jnp.split(cond @ w_mod[l] + b_mod[l], 6, axis=-1)
        mc = jnp.split((cond_ctx @ w_mod[l] + b_mod[l])[None, :], 6, axis=-1)
        mod = [jnp.stack([m[i], mc[i]], axis=0) for i in range(6)]

        w_main, w_small = _prep_w_in(w_in[l])
        pm, ps = inproj(xa, norm1[l][None, :], mod[0], mod[1], w_main, w_small, n_lat)

        neg_a, dt = gdn_gate_vectors(gdn_a_log[l], gdn_dt_bias[l])
        gq, gk, gv, gs = gdn_prep(pm, ps, gdn_conv[l], neg_a, dt, n_lat, n_tok)
        gdn_f, gdn_b = gdn_scan(gq, gk, gv, gs, n_lat, n_ctx)

        w_up, up_bias = gla_gate_weights(gla_gate_up[l], gla_gate_bias[l])
        gla_f, gla_b = gla_scan(pm, ps, w_up, up_bias, n_lat, n_ctx)

        lam = diff_lambda_scalar(diff_lambda[l], lam_init)
        q_gain = jnp.tile(diff_q_norm[l], 2 * DIFF_HEADS)[None, :]
        k_gain = jnp.tile(diff_k_norm[l], 2 * DIFF_HEADS)[None, :]
        d_gain = diff_out_norm[l][:, None]
        dq, dk, dvt = attn_prep(pm, cos, sin, q_gain, k_gain)
        d_lat = diff_attention(lam, dq, dk, dvt, d_gain, 1.0 - lam_init, n_lat, n_ctx, False)
        if with_ctx_out:
            d_ctx = diff_attention(lam, dq, dk, dvt, d_gain, 1.0 - lam_init, n_lat, n_ctx, True)
        else:
            d_ctx = jnp.zeros((n_ctx, GROUP_WIDTH), F32)

        w_router = jnp.pad(jnp.concatenate([router_group[l], router_expert[l]], axis=1),
                           ((0, 0), (0, ROUTER_WIDTH - N_GROUPS - N_EXPERTS)))
        pad_lanes = lambda t: jnp.pad(t, ((0, 0), (0, ROUTER_WIDTH - t.shape[1])))
        b_router = pad_lanes(jnp.concatenate([router_group_bias[l], router_expert_bias[l]])[None, :])
        xa, h2, routing, counts = mixproj(
            xa, pm, gdn_f, gdn_b, gla_f, gla_b, d_lat, d_ctx, w_out[l].astype(BF16), mod[2],
            jnp.tile(gdn_out_norm[l], GDN_HEADS)[None, :], jnp.tile(gla_out_norm[l], GLA_HEADS)[None, :],
            sc_conv[l], norm2[l][None, :], mod[3], mod[4], w_router, b_router, n_lat, n_ctx)

        n_moe = n_tok if with_ctx_out else n_lat
        y = hier_moe_pallas(h2[:n_moe], routing[:n_moe], counts[n_moe // MIX_TM - 1, 0],
                            expert_w1, expert_w3, expert_w2, l)
        xa = xa.at[:n_moe].add(jnp.where(is_lat[:n_moe], m[5], mc[5]) * y)
    return xa[:n_lat][None]
```

```python
import functools
import math

import jax
import jax.numpy as jnp
from jax import lax
from jax.experimental import pallas as pl
from jax.experimental.pallas import tpu as pltpu

D_MODEL = 2048
DEPTH = 2
GRID_W = 64
GROUP_WIDTH = 512
GDN_DK = 64
GDN_DV = 64
GDN_HEADS = GROUP_WIDTH // GDN_DV
SCAN_CHUNK = 64
SC_WIDTH = GROUP_WIDTH
GLA_DK = 64
GLA_DV = 128
GLA_HEADS = GROUP_WIDTH // GLA_DV
GLA_GATE_RANK = 16
GLA_TAU = 16.0
DIFF_DK = 64
DIFF_DV = 128
DIFF_HEADS = GROUP_WIDTH // DIFF_DV
ROPE_BASE = 10000.0
ROPE_PAIRS = DIFF_DK // 4
N_GROUPS = 4
EXPERTS_PER_GROUP = 8
N_EXPERTS = N_GROUPS * EXPERTS_PER_GROUP
TOP_K = 2
D_EXPERT = D_MODEL // 4
EXPERT_BLOCK = 256
NORM_EPS = 1e-6
F32 = jnp.float32
BF16 = jnp.bfloat16
HI = lax.Precision.HIGHEST

PROJ_LAYOUT = (
    ('gdn_qkv', GDN_HEADS * (2 * GDN_DK + GDN_DV)),
    ('gdn_gate', GDN_HEADS * GDN_DV),
    ('gdn_beta', 2 * GDN_HEADS),
    ('gdn_alpha', 2 * GDN_HEADS),
    ('sc_b', SC_WIDTH),
    ('sc_c', SC_WIDTH),
    ('sc_x', SC_WIDTH),
    ('gla_q', GLA_HEADS * GLA_DK),
    ('gla_k', GLA_HEADS * GLA_DK),
    ('gla_v', GLA_HEADS * GLA_DV),
    ('gla_r', GLA_HEADS * GLA_DV),
    ('gla_lowrank', 2 * GLA_GATE_RANK),
    ('diff_q', DIFF_HEADS * 2 * DIFF_DK),
    ('diff_k', DIFF_HEADS * 2 * DIFF_DK),
    ('diff_v', DIFF_HEADS * DIFF_DV),
)
PROJ_OFFSETS = {}
_off = 0
for _name, _size in PROJ_LAYOUT:
    PROJ_OFFSETS[_name] = (_off, _size)
    _off += _size

MAIN_GROUPS = ('gdn_qkv', 'gdn_gate', 'sc_b', 'sc_c', 'sc_x', 'gla_q', 'gla_k', 'gla_v', 'gla_r',
               'diff_q', 'diff_k', 'diff_v')
SMALL_GROUPS = ('gdn_beta', 'gdn_alpha', 'gla_lowrank')
MAIN_OFFSETS = {}
_off = 0
for _name in MAIN_GROUPS:
    MAIN_OFFSETS[_name] = (_off, PROJ_OFFSETS[_name][1])
    _off += PROJ_OFFSETS[_name][1]
MAIN_WIDTH = _off
SMALL_OFFSETS = {}
_off = 0
for _name in SMALL_GROUPS:
    SMALL_OFFSETS[_name] = (_off, PROJ_OFFSETS[_name][1])
    _off += PROJ_OFFSETS[_name][1]
LANE = 128
SMALL_WIDTH = LANE

VMEM_LIMIT = 48 * 1024 * 1024
ROW_TILE = 1024


INPROJ_TM = 1024
INPROJ_TN = 512


def _inproj_body(x_ref, gain_ref, shift_ref, scale_ref, w_ref, ws_ref, o_ref, os_ref, h_ref):
    @pl.when(pl.program_id(1) == 0)
    def _():
        x = x_ref[...]
        y = x * lax.rsqrt(jnp.mean(x * x, axis=-1, keepdims=True) + NORM_EPS) * gain_ref[...]
        hb = (y * (1.0 + scale_ref[...]) + shift_ref[...]).astype(BF16)
        h_ref[...] = hb
        os_ref[...] = jnp.dot(hb, ws_ref[...], preferred_element_type=F32)

    o_ref[...] = jnp.dot(h_ref[...], w_ref[...], preferred_element_type=F32)


def inproj(x, gain, shift, scale, w_main, w_small, n_lat):
    rows = x.shape[0]
    tm = INPROJ_TM
    lat_tiles = n_lat // tm
    mod = pl.BlockSpec((None, 1, D_MODEL), lambda i, j: (i // lat_tiles, 0, 0))
    return pl.pallas_call(
        _inproj_body,
        grid=(rows // tm, MAIN_WIDTH // INPROJ_TN),
        in_specs=[
            pl.BlockSpec((tm, D_MODEL), lambda i, j: (i, 0)),
            pl.BlockSpec((1, D_MODEL), lambda i, j: (0, 0)),
            mod, mod,
            pl.BlockSpec((D_MODEL, INPROJ_TN), lambda i, j: (0, j)),
            pl.BlockSpec((D_MODEL, SMALL_WIDTH), lambda i, j: (0, 0)),
        ],
        out_specs=[
            pl.BlockSpec((tm, INPROJ_TN), lambda i, j: (i, j)),
            pl.BlockSpec((tm, SMALL_WIDTH), lambda i, j: (i, 0)),
        ],
        out_shape=[
            jax.ShapeDtypeStruct((rows, MAIN_WIDTH), F32),
            jax.ShapeDtypeStruct((rows, SMALL_WIDTH), F32),
        ],
        scratch_shapes=[pltpu.VMEM((tm, D_MODEL), BF16)],
        compiler_params=pltpu.CompilerParams(
            dimension_semantics=("parallel", "arbitrary"), vmem_limit_bytes=VMEM_LIMIT),
        name="inproj",
    )(x, gain, shift, scale, w_main, w_small)


HEAD_LANES = 2 * DIFF_DK
ATTN_TK = 512
ATTN_TQ = 1024
BF16_SUBLANES = 16
DV_EXT = DIFF_DV + BF16_SUBLANES
LOG2E = math.log2(math.e)
NEG_BIG = -1e30


def _group_mean_sq(x, width):
    n = x.shape[-1]
    gi = lax.broadcasted_iota(jnp.int32, (n, n), 0) // width
    gj = lax.broadcasted_iota(jnp.int32, (n, n), 1) // width
    ones = jnp.where(gi == gj, 1.0, 0.0).astype(BF16)
    rest = x * x
    total = None
    for _ in range(3):
        term = rest.astype(BF16)
        part = jnp.dot(term, ones, preferred_element_type=F32)
        total = part if total is None else total + part
        rest = rest - term.astype(F32)
    return total * (1.0 / width)


def _rope_swap(x):
    n = x.shape[-1]
    lane = lax.broadcasted_iota(jnp.int32, x.shape, x.ndim - 1)
    return jnp.where((lane % (2 * ROPE_PAIRS)) < ROPE_PAIRS,
                     pltpu.roll(x, n - ROPE_PAIRS, 1), pltpu.roll(x, ROPE_PAIRS, 1))


def _attn_prep_body(q_ref, k_ref, v_ref, cos_ref, sin_ref, qg_ref, kg_ref, qo_ref, ko_ref, vt_ref):
    cos = jnp.concatenate([cos_ref[...]] * DIFF_HEADS, axis=-1)
    sin = jnp.concatenate([sin_ref[...]] * DIFF_HEADS, axis=-1)

    def norm_rope(x, gain):
        y = x * lax.rsqrt(_group_mean_sq(x, DIFF_DK) + NORM_EPS) * gain
        return y * cos + _rope_swap(y) * sin

    q = norm_rope(q_ref[...], qg_ref[...]) * (DIFF_DK ** -0.5 * LOG2E)
    lane = lax.broadcasted_iota(jnp.int32, q.shape, 1)
    first = (lane % HEAD_LANES) < DIFF_DK
    qo_ref[0] = jnp.where(first, q, 0.0).astype(BF16)
    qo_ref[1] = jnp.where(first, 0.0, q).astype(BF16)
    ko_ref[...] = norm_rope(k_ref[...], kg_ref[...]).astype(BF16)
    v = v_ref[...]
    ones_tile = jnp.where(lax.broadcasted_iota(jnp.int32, (BF16_SUBLANES, v.shape[0]), 0) == 0, 1.0, 0.0)
    for h in range(DIFF_HEADS):
        vt_ref[h, 0] = jnp.concatenate([v[:, h * DIFF_DV:(h + 1) * DIFF_DV].T, ones_tile], axis=0).astype(BF16)


def attn_prep(p_main, cos, sin, q_gain, k_gain):
    rows = p_main.shape[0]
    tm = ATTN_TK
    nb = rows // tm
    col = lambda name: MAIN_OFFSETS[name][0] // GROUP_WIDTH
    seg = lambda name: pl.BlockSpec((tm, GROUP_WIDTH), lambda i, c=col(name): (i, c))
    tab = pl.BlockSpec((tm, HEAD_LANES), lambda i: (i, 0))
    vec = pl.BlockSpec((1, GROUP_WIDTH), lambda i: (0, 0))
    return pl.pallas_call(
        _attn_prep_body,
        grid=(nb,),
        in_specs=[seg('diff_q'), seg('diff_k'), seg('diff_v'), tab, tab, vec, vec],
        out_specs=[
            pl.BlockSpec((2, tm, GROUP_WIDTH), lambda i: (0, i, 0)),
            pl.BlockSpec((tm, GROUP_WIDTH), lambda i: (i, 0)),
            pl.BlockSpec((DIFF_HEADS, 1, DV_EXT, tm), lambda i: (0, i, 0, 0)),
        ],
        out_shape=[
            jax.ShapeDtypeStruct((2, rows, GROUP_WIDTH), BF16),
            jax.ShapeDtypeStruct((rows, GROUP_WIDTH), BF16),
            jax.ShapeDtypeStruct((DIFF_HEADS, nb, DV_EXT, tm), BF16),
        ],
        compiler_params=pltpu.CompilerParams(dimension_semantics=("parallel",), vmem_limit_bytes=VMEM_LIMIT),
        name="attn_prep",
    )(p_main, p_main, p_main, cos, sin, q_gain, k_gain)


def _attn_body(lam_ref, q_ref, k_ref, vt_ref, gn_ref, o_ref, m_ref, acc_ref, st_ref, p_ref, *,
               n_lat_chunks, n_ctx, tq, post_scale):
    tk = ATTN_TK
    q2 = q_ref[...].reshape(2 * tq, HEAD_LANES)
    m_ref[...] = jnp.full(m_ref.shape, NEG_BIG, F32)
    acc_ref[...] = jnp.zeros(acc_ref.shape, F32)

    def scores(kc):
        return lax.dot_general(kc, q2, (((1,), (1,)), ((), ())), preferred_element_type=F32)

    def softmax_update(st, pv):
        m_prev = m_ref[...]
        m_new = jnp.maximum(m_prev, jnp.max(st, axis=0, keepdims=True))
        alpha = jnp.exp2(m_prev - m_new)
        acc_ref[...] = alpha * (acc_ref[...] + pv)
        m_ref[...] = m_new
        return jnp.exp2(st - m_new).astype(BF16)

    ctx0 = n_lat_chunks * tk
    p_ctx = softmax_update(scores(k_ref[ctx0:ctx0 + n_ctx, :]), 0.0)
    acc_ref[...] = acc_ref[...] + jnp.dot(vt_ref[n_lat_chunks][:, :n_ctx], p_ctx, preferred_element_type=F32)

    if n_lat_chunks:
        st_ref[0] = scores(k_ref[0:tk, :])
        p_ref[1] = jnp.zeros(p_ref.shape[1:], BF16)

        def half(j, cur, nxt):
            jn = jnp.minimum(j + 1, n_lat_chunks - 1)
            st_ref[nxt] = scores(k_ref[pl.ds(pl.multiple_of(jn * tk, tk), tk), :])
            pv = jnp.dot(vt_ref[jnp.maximum(j - 1, 0)], p_ref[nxt], preferred_element_type=F32)
            p_ref[cur] = softmax_update(st_ref[cur], pv)

        def body(i, carry):
            half(2 * i, 0, 1)
            half(2 * i + 1, 1, 0)
            return carry
        lax.fori_loop(0, n_lat_chunks // 2, body, 0)
        acc_ref[...] = acc_ref[...] + jnp.dot(vt_ref[n_lat_chunks - 1], p_ref[1], preferred_element_type=F32)

    acc = acc_ref[...]
    l = acc[DIFF_DV:DIFF_DV + 1]
    o = acc[:DIFF_DV, :tq] / l[:, :tq] - lam_ref[0, 0] * (acc[:DIFF_DV, tq:] / l[:, tq:])
    y = o * lax.rsqrt(jnp.mean(o * o, axis=0, keepdims=True) + NORM_EPS) * (gn_ref[...] * post_scale)
    o_ref[...] = y.T


def diff_attention(lam, q, k, vt, out_gain, post_scale, n_lat, n_ctx, context_queries):
    lat_chunks = n_lat // ATTN_TK
    if context_queries:
        rows, tq, q_off, n_lat_chunks = n_ctx, n_ctx, n_lat // n_ctx, 0
        k_spec = pl.BlockSpec((ATTN_TK, HEAD_LANES), lambda h, i: (lat_chunks, h))
        vt_spec = pl.BlockSpec((None, 1, DV_EXT, ATTN_TK), lambda h, i: (h, lat_chunks, 0, 0))
    else:
        assert lat_chunks % 2 == 0 and n_lat % ATTN_TQ == 0
        rows, tq, q_off, n_lat_chunks = n_lat, ATTN_TQ, 0, lat_chunks
        k_spec = pl.BlockSpec(((lat_chunks + 1) * ATTN_TK, HEAD_LANES), lambda h, i: (0, h))
        vt_spec = pl.BlockSpec((None, lat_chunks + 1, DV_EXT, ATTN_TK), lambda h, i: (h, 0, 0, 0))
    return pl.pallas_call(
        functools.partial(_attn_body, n_lat_chunks=n_lat_chunks, n_ctx=n_ctx, tq=tq, post_scale=post_scale),
        grid=(DIFF_HEADS, rows // tq),
        in_specs=[
            pl.BlockSpec(memory_space=pltpu.SMEM),
            pl.BlockSpec((2, tq, HEAD_LANES), lambda h, i: (0, q_off + i, h)),
            k_spec, vt_spec,
            pl.BlockSpec((DIFF_DV, 1), lambda h, i: (0, 0)),
        ],
        out_specs=pl.BlockSpec((tq, DIFF_DV), lambda h, i: (i, h)),
        out_shape=jax.ShapeDtypeStruct((rows, GROUP_WIDTH), F32),
        scratch_shapes=[
            pltpu.VMEM((1, 2 * tq), F32),
            pltpu.VMEM((DV_EXT, 2 * tq), F32),
            pltpu.VMEM((2, ATTN_TK, 2 * tq), F32),
            pltpu.VMEM((2, ATTN_TK, 2 * tq), BF16),
        ],
        compiler_params=pltpu.CompilerParams(
            dimension_semantics=("parallel", "arbitrary"), vmem_limit_bytes=VMEM_LIMIT),
        name="diff_attn_ctx" if context_queries else "diff_attn_lat",
    )(lam, q, k, vt, out_gain)


def rope_tables(n_lat, rows):
    pos = jnp.arange(n_lat, dtype=jnp.int32)
    row = (pos // GRID_W).astype(F32)
    col = (pos % GRID_W).astype(F32)
    inv_freq = ROPE_BASE ** (-jnp.arange(ROPE_PAIRS, dtype=F32) / ROPE_PAIRS)
    ang_r, ang_c = row[:, None] * inv_freq, col[:, None] * inv_freq
    cos = jnp.concatenate([jnp.cos(ang_r), jnp.cos(ang_r), jnp.cos(ang_c), jnp.cos(ang_c)], axis=-1)
    sin = jnp.concatenate([-jnp.sin(ang_r), jnp.sin(ang_r), -jnp.sin(ang_c), jnp.sin(ang_c)], axis=-1)
    cos, sin = jnp.tile(cos, (1, 2)), jnp.tile(sin, (1, 2))
    pad = ((0, rows - n_lat), (0, 0))
    return jnp.pad(cos, pad, constant_values=1.0), jnp.pad(sin, pad)


def diff_lambda_scalar(lam_params, lam_init):
    lq1, lk1, lq2, lk2 = lam_params
    return (jnp.exp(jnp.sum(lq1 * lk1)) - jnp.exp(jnp.sum(lq2 * lk2)) + lam_init).reshape(1, 1)


GDN_QKV_WIDTH = GDN_HEADS * (2 * GDN_DK + GDN_DV)
GDN_PREP_TM = 512
HALO = 8


def _softplus(z):
    return jnp.maximum(z, 0.0) + jnp.log(1.0 + jnp.exp(-jnp.abs(z)))


def _conv3(x, prev_row, next_row, w, row0, n_lat, n_tok):
    tm = x.shape[0]
    r = lax.broadcasted_iota(jnp.int32, (tm, 1), 0)
    rows = row0 + r
    xm1 = jnp.where(r == 0, prev_row, pltpu.roll(x, 1, 0))
    xm1 = jnp.where((rows == 0) | (rows == n_lat), 0.0, xm1)
    xp1 = jnp.where(r == tm - 1, next_row, pltpu.roll(x, tm - 1, 0))
    xp1 = jnp.where((rows == n_lat - 1) | (rows == n_tok - 1), 0.0, xp1)
    return xm1 * w[0:1] + x * w[1:2] + xp1 * w[2:3]


def _gdn_prep_body(x_ref, xp_ref, xn_ref, ps_ref, cw_ref, a_ref, dt_ref, q_ref, k_ref, v_ref, gs_ref, *, n_lat, n_tok):
    tm = x_ref.shape[0]
    y = _conv3(x_ref[...], xp_ref[HALO - 1:HALO, :], xn_ref[0:1, :], cw_ref[...], pl.program_id(0) * tm, n_lat, n_tok)
    y = y * jax.nn.sigmoid(y)
    hk = GDN_HEADS * GDN_DK
    q, k = y[:, :hk], y[:, hk:2 * hk]
    q_ref[...] = q * lax.rsqrt(_group_mean_sq(q, GDN_DK) * GDN_DK + NORM_EPS) * (GDN_DK ** -0.5)
    k_ref[...] = k * lax.rsqrt(_group_mean_sq(k, GDN_DK) * GDN_DK + NORM_EPS)
    v_ref[...] = y[:, 2 * hk:]
    ps = ps_ref[...]
    lane = lax.broadcasted_iota(jnp.int32, ps.shape, 1)
    gs_ref[...] = jnp.where(lane < 2 * GDN_HEADS, jax.nn.sigmoid(ps), a_ref[...] * _softplus(ps + dt_ref[...]))


def gdn_prep(p_main, p_small, conv_w, neg_a, dt, n_lat, n_tok):
    rows = p_main.shape[0]
    tm = GDN_PREP_TM
    nb = rows // tm
    per = tm // HALO
    out = jax.ShapeDtypeStruct((rows, GROUP_WIDTH), F32)
    return pl.pallas_call(
        functools.partial(_gdn_prep_body, n_lat=n_lat, n_tok=n_tok),
        grid=(nb,),
        in_specs=[
            pl.BlockSpec((tm, GDN_QKV_WIDTH), lambda i: (i, 0)),
            pl.BlockSpec((HALO, GDN_QKV_WIDTH), lambda i: (jnp.maximum(i * per - 1, 0), 0)),
            pl.BlockSpec((HALO, GDN_QKV_WIDTH), lambda i: (jnp.minimum((i + 1) * per, nb * per - 1), 0)),
            pl.BlockSpec((tm, SMALL_WIDTH), lambda i: (i, 0)),
            pl.BlockSpec((3, GDN_QKV_WIDTH), lambda i: (0, 0)),
            pl.BlockSpec((1, SMALL_WIDTH), lambda i: (0, 0)),
            pl.BlockSpec((1, SMALL_WIDTH), lambda i: (0, 0)),
        ],
        out_specs=[pl.BlockSpec((tm, GROUP_WIDTH), lambda i: (i, 0))] * 3 + [pl.BlockSpec((tm, SMALL_WIDTH), lambda i: (i, 0))],
        out_shape=[out, out, out, jax.ShapeDtypeStruct((rows, SMALL_WIDTH), F32)],
        compiler_params=pltpu.CompilerParams(dimension_semantics=("parallel",), vmem_limit_bytes=VMEM_LIMIT),
        name="gdn_prep",
    )(p_main, p_main, p_main, p_small, conv_w, neg_a, dt)


PAIR = 2 * SCAN_CHUNK
GDN_STEP_CHUNKS = 2


def _stack_pair(a):
    first = lax.broadcasted_iota(jnp.int32, a.shape, 1) < (PAIR // 2)
    return jnp.concatenate([jnp.where(first, a, 0.0), jnp.where(first, 0.0, a)], axis=0)


def _pair_col(arr, c0, c1):
    return jnp.concatenate([arr[:, c0:c0 + 1], arr[:, c1:c1 + 1]], axis=0)


def _dot_nt(a, b):
    return lax.dot_general(a, b, (((1,), (1,)), ((), ())), preferred_element_type=F32)


def _dot_tn(a, b):
    return lax.dot_general(a, b, (((0,), (0,)), ((), ())), preferred_element_type=F32)


def _chunk_masks(d):
    C = SCAN_CHUNK
    r = lax.broadcasted_iota(jnp.int32, (PAIR, PAIR), 0)
    c = lax.broadcasted_iota(jnp.int32, (PAIR, PAIR), 1)
    same = (r // C) == (c // C)
    ri, ci = r % C, c % C
    incl = same & ((ci <= ri) if d == 0 else (ci >= ri))
    strict = same & ((ci < ri) if d == 0 else (ci > ri))
    r1 = lax.broadcasted_iota(jnp.int32, (C, C), 0)
    c1 = lax.broadcasted_iota(jnp.int32, (C, C), 1)
    tri = jnp.where((c1 <= r1) if d == 0 else (c1 >= r1), 1.0, 0.0).astype(F32)
    return incl, strict, tri, r == c


def _gdn_scan_body(qf, kf, vf, gf, qb, kb, vb, gb, of_ref, ob_ref, s_ref):
    C = SCAN_CHUNK
    dot = functools.partial(jnp.dot, preferred_element_type=F32)

    @pl.when(pl.program_id(0) == 0)
    def _():
        s_ref[...] = jnp.zeros(s_ref.shape, F32)

    top = lax.broadcasted_iota(jnp.int32, (PAIR, 1), 0) < C
    units = []
    for d, (q_ref, k_ref, v_ref, g_ref, o_ref) in enumerate(((qf, kf, vf, gf, of_ref), (qb, kb, vb, gb, ob_ref))):
        incl, strict, tri, eye = _chunk_masks(d)
        for visit in range(GDN_STEP_CHUNKS):
            c = visit if d == 0 else GDN_STEP_CHUNKS - 1 - visit
            rs = slice(C * c, C * (c + 1))
            gs = g_ref[rs, :]
            g_all = jnp.dot(tri, gs, preferred_element_type=F32, precision=HI)
            g_end = g_all[C - 1:C] if d == 0 else g_all[0:1]
            g_t = g_all.T
            for p in range(GDN_HEADS // 2):
                sl = slice(PAIR * p, PAIR * (p + 1))
                b0 = GDN_HEADS * d + 2 * p
                c0 = 2 * GDN_HEADS + b0
                u = dict(d=d, p=p, sl=sl, rs=rs, visit=visit, o_ref=o_ref, incl=incl, strict=strict, eye=eye)
                u['K'], u['Q'], u['V'] = (_stack_pair(r[rs, sl]) for r in (k_ref, q_ref, v_ref))
                u['g_col'] = _pair_col(g_all, c0, c0 + 1)
                u['beta'] = _pair_col(gs, b0, b0 + 1)
                u['g_row'] = jnp.concatenate([g_t[c0:c0 + 1, :], g_t[c0 + 1:c0 + 2, :]], axis=1)
                u['gl_col'] = jnp.where(top, g_end[:, c0:c0 + 1], g_end[:, c0 + 1:c0 + 2])
                units.append(u)
    for u in units:
        u['decay'] = jnp.exp(jnp.where(u['incl'], u['g_col'] - u['g_row'], NEG_BIG))
        u['eg'] = jnp.exp(u['g_col'])
    for u in units:
        u['A'] = jnp.where(u['strict'], -(u['beta'] * _dot_nt(u['K'], u['K']) * u['decay']), 0.0)
        u['P'] = jnp.where(u['eye'], 1.0, 0.0) + u['A']
        u['negL'] = u['A']
    for u in units:
        u['attn'] = _dot_nt(u['Q'], u['K']) * u['decay']
    for _ in range(5):
        for u in units:
            u['A'] = dot(u['A'], u['A'])
        for u in units:
            u['P'] = u['P'] + dot(u['P'], u['A'])
    split = lambda t: (t.astype(BF16), (t - t.astype(BF16).astype(F32)).astype(BF16))
    for u in units:
        (l_hi, l_lo), (p_hi, p_lo) = split(u['negL']), split(u['P'])
        u['R'] = (jnp.where(u['eye'], 1.0, 0.0) - u['P']) + (dot(l_hi, p_hi) + (dot(l_hi, p_lo) + dot(l_lo, p_hi)))
    for u in units:
        u['P'] = u['P'] + dot(u['P'], u['R'])
    for u in units:
        u['uw'] = dot(u['P'], jnp.concatenate([u['V'] * u['beta'], u['K'] * (u['beta'] * u['eg'])], axis=1))
    state = {(d, p): s_ref[d, p] for d in range(2) for p in range(GDN_HEADS // 2)}
    for visit in range(GDN_STEP_CHUNKS):
        now = [u for u in units if u['visit'] == visit]
        for u in now:
            u['S'] = state[u['d'], u['p']]
            u['ws'] = dot(jnp.concatenate([u['uw'][:, PAIR:], u['Q'] * u['eg']], axis=0), u['S'])
        for u in now:
            u['v_new'] = u['uw'][:, :PAIR] - u['ws'][:PAIR]
            o_st = u['ws'][PAIR:] + dot(u['attn'], u['v_new'])
            u['o_ref'][u['rs'], u['sl']] = o_st[:C] + o_st[C:]
        for u in now:
            state[u['d'], u['p']] = (u['S'] * jnp.exp(u['gl_col'])
                                     + _dot_tn(u['K'] * jnp.exp(u['gl_col'] - u['g_col']), u['v_new']))
    for (d, p), s_new in state.items():
        s_ref[d, p] = s_new


def _scan_order(n_lat, n_ctx, block_rows=SCAN_CHUNK):
    assert n_lat % block_rows == 0 and n_ctx % block_rows == 0
    nl, nc = n_lat // block_rows, n_ctx // block_rows
    fwd = lambda t: (jnp.where(t < nc, nl + t, t - nc), 0)
    bwd = lambda t: (nl + nc - 1 - t, 0)
    return nl + nc, fwd, bwd


def gdn_scan(q, k, v, gs, n_lat, n_ctx):
    rows = GDN_STEP_CHUNKS * SCAN_CHUNK
    steps, fwd, bwd = _scan_order(n_lat, n_ctx, rows)
    wide = lambda f: pl.BlockSpec((rows, GROUP_WIDTH), f)
    small = lambda f: pl.BlockSpec((rows, SMALL_WIDTH), f)
    out = jax.ShapeDtypeStruct((n_lat + n_ctx, GROUP_WIDTH), F32)
    return pl.pallas_call(
        _gdn_scan_body,
        grid=(steps,),
        in_specs=[wide(fwd), wide(fwd), wide(fwd), small(fwd), wide(bwd), wide(bwd), wide(bwd), small(bwd)],
        out_specs=[wide(fwd), wide(bwd)],
        out_shape=[out, out],
        scratch_shapes=[pltpu.VMEM((2, GDN_HEADS // 2, PAIR, PAIR), F32)],
        compiler_params=pltpu.CompilerParams(dimension_semantics=("arbitrary",), vmem_limit_bytes=VMEM_LIMIT),
        name="gdn_scan",
    )(q, k, v, gs, q, k, v, gs)


def gdn_gate_vectors(a_log, dt_bias):
    n = 2 * GDN_HEADS
    pad = lambda t: jnp.pad(t.reshape(1, n), ((0, 0), (n, SMALL_WIDTH - 2 * n)))
    return pad(-jnp.exp(a_log)), pad(dt_bias)


GLA_QK_WIDTH = GLA_HEADS * GLA_DK


def _gla_scan_body(qf, kf, vf, pf, qb, kb, vb, pb, wup_ref, bias_ref, of_ref, ob_ref, s_ref):
    C = SCAN_CHUNK
    dot = functools.partial(jnp.dot, preferred_element_type=F32)

    @pl.when(pl.program_id(0) == 0)
    def _():
        s_ref[...] = jnp.zeros(s_ref.shape, F32)

    units = []
    for d, (q_ref, k_ref, v_ref, p_ref, o_ref) in enumerate(((qf, kf, vf, pf, of_ref), (qb, kb, vb, pb, ob_ref))):
        incl, _, tri, _ = _chunk_masks(d)
        z = dot(p_ref[...], wup_ref[...]) + bias_ref[...]
        z = z[:, GLA_QK_WIDTH * d:GLA_QK_WIDTH * (d + 1)]
        log_a = (jnp.minimum(z, 0.0) - jnp.log(1.0 + jnp.exp(-jnp.abs(z)))) * (1.0 / GLA_TAU)
        b = jnp.dot(tri, log_a, preferred_element_type=F32, precision=HI)
        b_end = b[C - 1:C] if d == 0 else b[0:1]
        q = q_ref[...] * (GLA_DK ** -0.5)
        k_out = k_ref[...] * jnp.exp(b_end - b)
        q_in = q * jnp.exp(b - b_end)
        q_st = q * jnp.exp(b)
        a_col = jnp.broadcast_to(jnp.exp(b_end), (C, GLA_QK_WIDTH)).T[:, 0:1]
        v = v_ref[...]
        for p in range(GLA_HEADS // 2):
            sl = slice(PAIR * p, PAIR * (p + 1))
            v0, v1 = v[:, 2 * p * GLA_DV:(2 * p + 1) * GLA_DV], v[:, (2 * p + 1) * GLA_DV:(2 * p + 2) * GLA_DV]
            units.append(dict(d=d, p=p, o_ref=o_ref, incl=incl, V=jnp.concatenate([v0, v1], axis=0),
                              K=_stack_pair(k_out[:, sl]), Qin=_stack_pair(q_in[:, sl]), Qst=_stack_pair(q_st[:, sl]),
                              a=a_col[sl]))
    for u in units:
        u['attn'] = jnp.where(u['incl'], _dot_nt(u['Qin'], u['K']), 0.0)
        u['S'] = s_ref[u['d'], u['p']]
    for u in units:
        u['o'] = dot(u['Qst'], u['S'])
        u['dS'] = _dot_tn(u['K'], u['V'])
    for u in units:
        o_st = u['o'] + dot(u['attn'], u['V'])
        p = u['p']
        u['o_ref'][:, 2 * p * GLA_DV:(2 * p + 1) * GLA_DV] = o_st[:C]
        u['o_ref'][:, (2 * p + 1) * GLA_DV:(2 * p + 2) * GLA_DV] = o_st[C:]
        s_ref[u['d'], p] = u['a'] * u['S'] + u['dS']


def gla_scan(p_main, p_small, w_up, bias, n_lat, n_ctx):
    steps, fwd, bwd = _scan_order(n_lat, n_ctx)
    qc, kc, vc = (MAIN_OFFSETS[n][0] for n in ('gla_q', 'gla_k', 'gla_v'))
    at = lambda f, blk: (lambda t: (f(t)[0], blk))
    qk = lambda f, off: pl.BlockSpec((SCAN_CHUNK, GLA_QK_WIDTH), at(f, off // GLA_QK_WIDTH))
    vv = lambda f: pl.BlockSpec((SCAN_CHUNK, GROUP_WIDTH), at(f, vc // GROUP_WIDTH))
    small = lambda f: pl.BlockSpec((SCAN_CHUNK, SMALL_WIDTH), f)
    const = lambda shape: pl.BlockSpec(shape, lambda t: (0, 0))
    out = jax.ShapeDtypeStruct((n_lat + n_ctx, GROUP_WIDTH), F32)
    return pl.pallas_call(
        _gla_scan_body,
        grid=(steps,),
        in_specs=[qk(fwd, qc), qk(fwd, kc), vv(fwd), small(fwd), qk(bwd, qc), qk(bwd, kc), vv(bwd), small(bwd),
                  const((SMALL_WIDTH, 2 * GLA_QK_WIDTH)), const((1, 2 * GLA_QK_WIDTH))],
        out_specs=[pl.BlockSpec((SCAN_CHUNK, GROUP_WIDTH), fwd), pl.BlockSpec((SCAN_CHUNK, GROUP_WIDTH), bwd)],
        out_shape=[out, out],
        scratch_shapes=[pltpu.VMEM((2, GLA_HEADS // 2, PAIR, GLA_DV), F32)],
        compiler_params=pltpu.CompilerParams(dimension_semantics=("arbitrary",), vmem_limit_bytes=VMEM_LIMIT),
        name="gla_scan",
    )(p_main, p_main, p_main, p_small, p_main, p_main, p_main, p_small, w_up, bias)


def gla_gate_weights(gate_up, gate_bias):
    off = SMALL_OFFSETS['gla_lowrank'][0]
    w = jnp.zeros((SMALL_WIDTH, 2 * GLA_QK_WIDTH), F32)
    for d in range(2):
        w = w.at[off + d * GLA_GATE_RANK:off + (d + 1) * GLA_GATE_RANK, d * GLA_QK_WIDTH:(d + 1) * GLA_QK_WIDTH].set(gate_up[d])
    return w, gate_bias.reshape(1, 2 * GLA_QK_WIDTH)


MIX_TM = 256
ROUTER_WIDTH = LANE


def _mixproj_body(x_ref, gf_ref, gb_ref, gate_ref, scb_ref, scc_ref, scx_ref, ccp_ref, ccn_ref, cxp_ref, cxn_ref,
                  lf_ref, lb_ref, r_ref, dl_ref, dc_ref, w_ref, g2_ref, gng_ref, gnl_ref, scw_ref, n2_ref,
                  shift_ref, scale_ref, wrh_ref, wrl_ref, rb_ref, xo_ref, h2_ref, rt_ref, co_ref, cnt_ref, *,
                  n_lat, n_tok):
    tm = x_ref.shape[0]
    row0 = pl.program_id(0) * tm
    silu = lambda t: t * jax.nn.sigmoid(t)
    dot = functools.partial(jnp.dot, preferred_element_type=F32)

    og = gf_ref[...] + gb_ref[...]
    a = og * lax.rsqrt(_group_mean_sq(og, GDN_DV) + NORM_EPS) * gng_ref[...] * silu(gate_ref[...])
    u = scc_ref[...] * scx_ref[...]
    u_prev = ccp_ref[HALO - 1:HALO, :] * cxp_ref[HALO - 1:HALO, :]
    u_next = ccn_ref[0:1, :] * cxn_ref[0:1, :]
    b = scb_ref[...] * _conv3(u, u_prev, u_next, scw_ref[...], row0, n_lat, n_tok)
    ol = lf_ref[...] + lb_ref[...]
    g = ol * lax.rsqrt(_group_mean_sq(ol, GLA_DV) + NORM_EPS) * gnl_ref[...] * silu(r_ref[...])
    d = jnp.where(row0 < n_lat, dl_ref[...], dc_ref[...])

    W = GROUP_WIDTH
    y = (dot(a.astype(BF16), w_ref[0:W, :]) + dot(b.astype(BF16), w_ref[W:2 * W, :])
         + dot(g.astype(BF16), w_ref[2 * W:3 * W, :]) + dot(d.astype(BF16), w_ref[3 * W:4 * W, :]))
    xn = x_ref[...] + g2_ref[...] * y
    xo_ref[...] = xn
    h2 = xn * lax.rsqrt(jnp.mean(xn * xn, axis=-1, keepdims=True) + NORM_EPS) * n2_ref[...]
    h2 = h2 * (1.0 + scale_ref[...]) + shift_ref[...]
    hb = h2.astype(BF16)
    h2_ref[...] = h2
    h_lo = (h2 - hb.astype(F32)).astype(BF16)
    logits = dot(hb, wrh_ref[...]) + (dot(h_lo, wrh_ref[...]) + dot(hb, wrl_ref[...])) + rb_ref[...]
    @pl.when(pl.program_id(0) == 0)
    def _():
        cnt_ref[...] = jnp.zeros(cnt_ref.shape, F32)

    routing, counts = _route(logits, cnt_ref[...])
    rt_ref[...] = routing
    cnt_ref[...] = counts
    co_ref[...] = jnp.broadcast_to(counts, co_ref.shape)


def _route(lg, counts):
    lane = lax.broadcasted_iota(jnp.int32, lg.shape, 1)
    big = 4 * LANE
    rmax = lambda t: jnp.max(t, axis=1, keepdims=True)
    rsum = lambda t: jnp.sum(t, axis=1, keepdims=True)
    first_lane = lambda cond: jnp.min(jnp.where(cond, lane, big), axis=1, keepdims=True)

    is_g = lane < N_GROUPS
    g_exp = jnp.where(is_g, jnp.exp(lg - rmax(jnp.where(is_g, lg, NEG_BIG))), 0.0)
    g_prob = g_exp / rsum(g_exp)
    g_top = rmax(g_prob)
    g_idx = first_lane(is_g & (g_prob == g_top))

    e_lane = lane - N_GROUPS
    in_g = (e_lane >= 0) & (e_lane < N_EXPERTS) & ((e_lane // EXPERTS_PER_GROUP) == g_idx)
    e_exp = jnp.where(in_g, jnp.exp(lg - rmax(jnp.where(in_g, lg, NEG_BIG))), 0.0)
    e_prob = e_exp / rsum(e_exp)
    p1 = rmax(jnp.where(in_g, e_prob, -1.0))
    i1 = first_lane(in_g & (e_prob == p1))
    rest = in_g & (lane != i1)
    p2 = rmax(jnp.where(rest, e_prob, -1.0))
    i2 = first_lane(rest & (e_prob == p2))
    w1 = g_top * (p1 / (p1 + p2))
    w2 = g_top * (p2 / (p1 + p2))
    ids = jnp.where(lane == 0, i1 - N_GROUPS, i2 - N_GROUPS).astype(F32)
    tm = lg.shape[0]
    hit1, hit2 = lane == i1, lane == i2
    onehot = jnp.where(hit1 | hit2, 1.0, 0.0)
    earlier = jnp.where(lax.broadcasted_iota(jnp.int32, (tm, tm), 1) < lax.broadcasted_iota(jnp.int32, (tm, tm), 0),
                        1.0, 0.0).astype(BF16)
    before = jnp.dot(earlier, onehot.astype(BF16), preferred_element_type=F32) + counts
    r1 = rsum(jnp.where(hit1, before, 0.0))
    r2 = rsum(jnp.where(hit2, before, 0.0))
    vals = (ids, ids, w1, w2, r1, r2)
    routing = jnp.zeros(lg.shape, F32)
    for i, v in enumerate(vals):
        routing = jnp.where(lane == i, v, routing)
    return routing, counts + jnp.sum(onehot, axis=0, keepdims=True)


def mixproj(xa, pm, gdn_f, gdn_b, gla_f, gla_b, d_lat, d_ctx, w_out, gate, gdn_gain, gla_gain, sc_w, norm2,
            shift, scale, w_router, b_router, n_lat, n_ctx):
    w_hi = w_router.astype(BF16)
    w_lo = (w_router - w_hi.astype(F32)).astype(BF16)
    rows = xa.shape[0]
    n_tok = n_lat + n_ctx
    tm = MIX_TM
    assert n_lat % tm == 0 and n_ctx == tm
    lat_tiles = n_lat // tm
    per = tm // HALO
    last_halo = rows // HALO - 1
    seg = lambda name: pl.BlockSpec((tm, GROUP_WIDTH), lambda i, c=MAIN_OFFSETS[name][0] // GROUP_WIDTH: (i, c))
    halo_prev = lambda name: pl.BlockSpec(
        (HALO, GROUP_WIDTH), lambda i, c=MAIN_OFFSETS[name][0] // GROUP_WIDTH: (jnp.maximum(i * per - 1, 0), c))
    halo_next = lambda name: pl.BlockSpec(
        (HALO, GROUP_WIDTH), lambda i, c=MAIN_OFFSETS[name][0] // GROUP_WIDTH: (jnp.minimum((i + 1) * per, last_halo), c))
    tok = pl.BlockSpec((tm, GROUP_WIDTH), lambda i: (i, 0))
    full = pl.BlockSpec((tm, D_MODEL), lambda i: (i, 0))
    mod = pl.BlockSpec((None, 1, D_MODEL), lambda i: (i // lat_tiles, 0, 0))
    const = lambda shape: pl.BlockSpec(shape, lambda i: (0,) * len(shape))
    return pl.pallas_call(
        functools.partial(_mixproj_body, n_lat=n_lat, n_tok=n_tok),
        grid=(n_tok // tm,),
        in_specs=[
            full, tok, tok, seg('gdn_gate'), seg('sc_b'), seg('sc_c'), seg('sc_x'),
            halo_prev('sc_c'), halo_next('sc_c'), halo_prev('sc_x'), halo_next('sc_x'),
            tok, tok, seg('gla_r'),
            pl.BlockSpec((tm, GROUP_WIDTH), lambda i: (jnp.minimum(i, lat_tiles - 1), 0)),
            pl.BlockSpec((tm, GROUP_WIDTH), lambda i: (0, 0)),
            const((D_MODEL, D_MODEL)), mod, const((1, GROUP_WIDTH)), const((1, GROUP_WIDTH)), const((3, GROUP_WIDTH)),
            const((1, D_MODEL)), mod, mod, const((D_MODEL, ROUTER_WIDTH)), const((D_MODEL, ROUTER_WIDTH)),
            const((1, ROUTER_WIDTH)),
        ],
        out_specs=[full, full, pl.BlockSpec((tm, ROUTER_WIDTH), lambda i: (i, 0)),
                   pl.BlockSpec((None, HALO, ROUTER_WIDTH), lambda i: (i, 0, 0))],
        out_shape=[
            jax.ShapeDtypeStruct((rows, D_MODEL), F32),
            jax.ShapeDtypeStruct((n_tok, D_MODEL), F32),
            jax.ShapeDtypeStruct((n_tok, ROUTER_WIDTH), F32),
            jax.ShapeDtypeStruct((n_tok // tm, HALO, ROUTER_WIDTH), F32),
        ],
        scratch_shapes=[pltpu.VMEM((1, ROUTER_WIDTH), F32)],
        input_output_aliases={0: 0},
        compiler_params=pltpu.CompilerParams(dimension_semantics=("arbitrary",), vmem_limit_bytes=VMEM_LIMIT),
        name="mixproj",
    )(xa, gdn_f, gdn_b, pm, pm, pm, pm, pm, pm, pm, pm, gla_f, gla_b, pm, d_lat, d_ctx, w_out, gate,
      gdn_gain, gla_gain, sc_w, norm2, shift, scale, w_hi, w_lo, b_router)


def _ffn_body(be_ref, nu_ref, x_ref, w1_ref, w3_ref, w2_ref, o_ref, w1b_ref, w3b_ref, w2b_ref):
    b = pl.program_id(0)
    used = b < nu_ref[0]
    new_expert = jnp.logical_or(b == 0, be_ref[b] != be_ref[jnp.maximum(b - 1, 0)])

    @pl.when(jnp.logical_and(used, new_expert))
    def _():
        w1b_ref[...] = w1_ref[...].astype(BF16)
        w3b_ref[...] = w3_ref[...].astype(BF16)
        w2b_ref[...] = w2_ref[...].astype(BF16)

    @pl.when(used)
    def _():
        x = x_ref[...].astype(BF16)
        a = jnp.dot(x, w1b_ref[...], preferred_element_type=F32)
        g = jnp.dot(x, w3b_ref[...], preferred_element_type=F32)
        mid = (a * jax.nn.sigmoid(a) * g).astype(BF16)
        o_ref[...] = jnp.dot(mid, w2b_ref[...], preferred_element_type=F32)

    @pl.when(jnp.logical_not(used))
    def _():
        o_ref[...] = jnp.zeros(o_ref.shape, F32)


def expert_ffn(block_expert, n_used, xs, w1, w3, w2, layer):
    L = xs.shape[0]
    n_blocks = L // EXPERT_BLOCK
    wspec = lambda shape: pl.BlockSpec((None, None) + shape, lambda b, be, nu: (layer, be[b], 0, 0))
    return pl.pallas_call(
        _ffn_body,
        grid_spec=pltpu.PrefetchScalarGridSpec(
            num_scalar_prefetch=2,
            grid=(n_blocks,),
            in_specs=[
                pl.BlockSpec((EXPERT_BLOCK, D_MODEL), lambda b, be, nu: (b, 0)),
                wspec((D_MODEL, D_EXPERT)), wspec((D_MODEL, D_EXPERT)), wspec((D_EXPERT, D_MODEL)),
            ],
            out_specs=pl.BlockSpec((EXPERT_BLOCK, D_MODEL), lambda b, be, nu: (b, 0)),
            scratch_shapes=[
                pltpu.VMEM((D_MODEL, D_EXPERT), BF16),
                pltpu.VMEM((D_MODEL, D_EXPERT), BF16),
                pltpu.VMEM((D_EXPERT, D_MODEL), BF16),
            ],
        ),
        out_shape=jax.ShapeDtypeStruct((L, D_MODEL), F32),
        compiler_params=pltpu.CompilerParams(dimension_semantics=("arbitrary",), vmem_limit_bytes=VMEM_LIMIT),
        name="expert_ffn",
    )(block_expert, n_used, xs, w1, w3, w2)


def hier_moe_pallas(hb, routing, counts, w1, w3, w2, layer):
    N, D = hb.shape
    E, M, K = N_EXPERTS, EXPERT_BLOCK, TOP_K
    expert_idx = routing[:, 0:K].astype(jnp.int32)
    weights = routing[:, K:2 * K]
    rank = routing[:, 2 * K:3 * K].astype(jnp.int32)
    counts = counts[N_GROUPS:N_GROUPS + E].astype(jnp.int32)
    padded = (counts + M - 1) // M * M
    pad_end = jnp.cumsum(padded)
    pad_start = pad_end - padded
    pos = pad_start[expert_idx] + rank
    n_blocks = -(-(N * K) // M) + E
    L = n_blocks * M
    token = jnp.broadcast_to(jnp.arange(N, dtype=jnp.int32)[:, None], (N, K))
    slot_tok = jnp.full((L,), N, jnp.int32).at[pos.reshape(-1)].set(token.reshape(-1))
    block_row0 = jnp.arange(n_blocks, dtype=jnp.int32)[:, None] * M
    block_expert = jnp.minimum(jnp.sum((pad_end[None, :] <= block_row0).astype(jnp.int32), axis=1), E - 1)
    n_used = (pad_end[-1] // M).astype(jnp.int32).reshape(1)
    h_pad = jnp.concatenate([hb, jnp.zeros((1, D), F32)], axis=0)
    xs = h_pad[slot_tok]
    y = expert_ffn(block_expert, n_used, xs, w1, w3, w2, layer)
    return y[pos[:, 0]] * weights[:, 0:1] + y[pos[:, 1]] * weights[:, 1:2]


def _prep_w_in(w):
    cols = lambda names: jnp.concatenate(
        [w[:, PROJ_OFFSETS[n][0]:PROJ_OFFSETS[n][0] + PROJ_OFFSETS[n][1]] for n in names], axis=1)
    small = cols(SMALL_GROUPS)
    small = jnp.pad(small, ((0, 0), (0, SMALL_WIDTH - small.shape[1])))
    return cols(MAIN_GROUPS).astype(BF16), small.astype(BF16)


def kernel(x, c, ctx, c_ctx, w_mod, b_mod, norm1, norm2, w_in, w_out, gdn_conv, gdn_a_log, gdn_dt_bias, gdn_out_norm, sc_conv, gla_gate_up, gla_gate_bias, gla_out_norm, diff_q_norm, diff_k_norm, diff_lambda, diff_out_norm, router_group, router_group_bias, router_expert, router_expert_bias, expert_w1, expert_w3, expert_w2):
    n_lat, n_ctx = x.shape[1], ctx.shape[1]
    n_tok = n_lat + n_ctx
    rows = -(-n_tok // ROW_TILE) * ROW_TILE
    pad_rows = lambda t: jnp.pad(t, ((0, rows - t.shape[0]), (0, 0)))
    cos, sin = rope_tables(n_lat, rows)
    is_lat = jnp.arange(rows, dtype=jnp.int32)[:, None] < n_lat

    xa = pad_rows(jnp.concatenate([x[0], ctx[0]], axis=0))
    cond, cond_ctx = jax.nn.silu(c), jax.nn.silu(c_ctx)
    for l in range(DEPTH):
        with_ctx_out = l < DEPTH - 1
        lam_init = 0.8 - 0.6 * math.exp(-0.3 * l)
        m = jnp.split(cond @ w_mod[l] + b_mod[l], 6, axis=-1)
        mc = jnp.split((cond_ctx @ w_mod[l] + b_mod[l])[None, :], 6, axis=-1)
        mod = [jnp.stack([m[i], mc[i]], axis=0) for i in range(6)]

        w_main, w_small = _prep_w_in(w_in[l])
        pm, ps = inproj(xa, norm1[l][None, :], mod[0], mod[1], w_main, w_small, n_lat)

        neg_a, dt = gdn_gate_vectors(gdn_a_log[l], gdn_dt_bias[l])
        gq, gk, gv, gs = gdn_prep(pm, ps, gdn_conv[l], neg_a, dt, n_lat, n_tok)
        gdn_f, gdn_b = gdn_scan(gq, gk, gv, gs, n_lat, n_ctx)

        w_up, up_bias = gla_gate_weights(gla_gate_up[l], gla_gate_bias[l])
        gla_f, gla_b = gla_scan(pm, ps, w_up, up_bias, n_lat, n_ctx)

        lam = diff_lambda_scalar(diff_lambda[l], lam_init)
        q_gain = jnp.tile(diff_q_norm[l], 2 * DIFF_HEADS)[None, :]
        k_gain = jnp.tile(diff_k_norm[l], 2 * DIFF_HEADS)[None, :]
        d_gain = diff_out_norm[l][:, None]
        dq, dk, dvt = attn_prep(pm, cos, sin, q_gain, k_gain)
        d_lat = diff_attention(lam, dq, dk, dvt, d_gain, 1.0 - lam_init, n_lat, n_ctx, False)
        if with_ctx_out:
            d_ctx = diff_attention(lam, dq, dk, dvt, d_gain, 1.0 - lam_init, n_lat, n_ctx, True)
        else:
            d_ctx = jnp.zeros((n_ctx, GROUP_WIDTH), F32)

        w_router = jnp.pad(jnp.concatenate([router_group[l], router_expert[l]], axis=1),
                           ((0, 0), (0, ROUTER_WIDTH - N_GROUPS - N_EXPERTS)))
        pad_lanes = lambda t: jnp.pad(t, ((0, 0), (0, ROUTER_WIDTH - t.shape[1])))
        b_router = pad_lanes(jnp.concatenate([router_group_bias[l], router_expert_bias[l]])[None, :])
        xa, h2, routing, counts = mixproj(
            xa, pm, gdn_f, gdn_b, gla_f, gla_b, d_lat, d_ctx, w_out[l].astype(BF16), mod[2],
            jnp.tile(gdn_out_norm[l], GDN_HEADS)[None, :], jnp.tile(gla_out_norm[l], GLA_HEADS)[None, :],
            sc_conv[l], norm2[l][None, :], mod[3], mod[4], w_router, b_router, n_lat, n_ctx)

        n_moe = n_tok if with_ctx_out else n_lat
        y = hier_moe_pallas(h2[:n_moe], routing[:n_moe], counts[n_moe // MIX_TM - 1, 0],
                            expert_w1, expert_w3, expert_w2, l)
        xa = xa.at[:n_moe].add(jnp.where(is_lat[:n_moe], m[5], mc[5]) * y)
    return xa[:n_lat][None]
```
